```python
import jax
import jax.numpy as jnp
from jax import lax
import numpy as np

D_MODEL = 1024
BATCH = 4
SEQ = 8192
DEPTH = 2

CTX_LEN = 256
GRID_W = 64
N_BRANCH = 3
BRANCH_W = 512
MLA_HEADS = 8
MLA_NOPE = 64
MLA_ROPE = 32
MLA_QK = MLA_NOPE + MLA_ROPE
MLA_V = BRANCH_W // MLA_HEADS
MLA_Q_LORA = 256
MLA_KV_LORA = 128
GLA_HEADS = 4
GLA_DK = (D_MODEL // 2) // GLA_HEADS
GLA_DV = BRANCH_W // GLA_HEADS
GLA_GATE_RANK = 16
GLA_GATE_NORMALIZER = 16.0
RET_HEADS = 4
RET_DK = BRANCH_W // RET_HEADS
RET_DV = BRANCH_W // RET_HEADS
D_FF = 2816
CONV_W = 3
CHUNK = 64
Q_BLOCK = 128
ROPE_THETA = 10000.0
RET_THETA = 10000.0
EPS = 1e-6
F32 = jnp.float32

IN_LAYOUT = (
    ('mla_q', MLA_Q_LORA),
    ('mla_kv', MLA_KV_LORA),
    ('mla_kr', MLA_ROPE),
    ('gla_q', GLA_HEADS * GLA_DK),
    ('gla_k', GLA_HEADS * GLA_DK),
    ('gla_v', GLA_HEADS * GLA_DV),
    ('gla_g', GLA_HEADS * GLA_DV),
    ('gla_rf', GLA_GATE_RANK),
    ('gla_rb', GLA_GATE_RANK),
    ('ret_q', RET_HEADS * RET_DK),
    ('ret_k', RET_HEADS * RET_DK),
    ('ret_v', RET_HEADS * RET_DV),
    ('ret_g', RET_HEADS * RET_DV),
    ('gate_mla', D_MODEL),
    ('gate_gla', D_MODEL),
    ('gate_ret', D_MODEL),
)
N_IN = sum(width for _, width in IN_LAYOUT)
CTX_SIDE = ('mla_kv', 'mla_kr', 'gla_k', 'gla_v', 'gla_rf', 'gla_rb', 'ret_k', 'ret_v')
QUERY_SIDE = ('mla_q', 'gla_q', 'gla_g', 'ret_q', 'ret_g', 'gate_mla', 'gate_gla', 'gate_ret')

kernel_name = 'hybrid_mla_gla_retention_dit_block'


def rms_norm(x, w=None):
    x32 = x.astype(F32)
    y = x32 * lax.rsqrt(jnp.mean(x32 * x32, axis=-1, keepdims=True) + EPS)
    if w is not None:
        y = y * w.astype(F32)
    return y.astype(x.dtype)


def modulate(x, shift, scale):
    return x * (1 + scale) + shift


def split_heads(t, n_heads):
    b, s, _ = t.shape
    return t.reshape(b, s, n_heads, -1).transpose(0, 2, 1, 3)


def merge_heads(t):
    b, h, s, d = t.shape
    return t.transpose(0, 2, 1, 3).reshape(b, s, h * d)


def in_proj(a, w, names):
    out, start = {}, 0
    for name, width in IN_LAYOUT:
        if name in names:
            out[name] = a @ w[:, start:start + width]
        start += width
    return out


def rope_tables(pos, dim, theta):
    inv = theta ** (-jnp.arange(dim // 2, dtype=F32) * 2.0 / dim)
    ang = pos.astype(F32)[:, None] * inv[None, :]
    return jnp.cos(ang), jnp.sin(ang)


def retention_tables(pos):
    inv = 1.0 / (RET_THETA ** jnp.linspace(0.0, 1.0, RET_DK // 2, dtype=F32))
    ang = pos.astype(F32)[:, None] * inv[None, :]
    return jnp.cos(ang), jnp.sin(ang)


def rotate_half(x, cos, sin):
    n = x.shape[-1] // 2
    x1, x2 = x[..., :n], x[..., n:]
    return jnp.concatenate([x1 * cos - x2 * sin, x1 * sin + x2 * cos], axis=-1).astype(x.dtype)


def axial_rope(x, tabs):
    cos_r, sin_r, cos_c, sin_c = tabs
    half = x.shape[-1] // 2
    return jnp.concatenate([rotate_half(x[..., :half], cos_r, sin_r),
                            rotate_half(x[..., half:], cos_c, sin_c)], axis=-1)


def mla_rope(t, tabs):
    if tabs is None:
        return t
    return jnp.concatenate([t[..., :MLA_NOPE], axial_rope(t[..., MLA_NOPE:], tabs)], axis=-1)


def mla_queries(cq, q_norm_a, w_qb, q_norm, tabs):
    q = split_heads(rms_norm(cq, q_norm_a) @ w_qb, MLA_HEADS)
    return mla_rope(rms_norm(q, q_norm), tabs)


def mla_keys_values(ckv, kr, kv_norm_a, w_kvb, k_norm, tabs):
    kv = split_heads(rms_norm(ckv, kv_norm_a) @ w_kvb, MLA_HEADS)
    b, h, t, _ = kv.shape
    k_rope = jnp.broadcast_to(kr[:, None], (b, h, t, MLA_ROPE))
    k = rms_norm(jnp.concatenate([kv[..., :MLA_NOPE], k_rope], axis=-1), k_norm)
    return mla_rope(k, tabs), kv[..., MLA_NOPE:]


def attend(q, k, v):
    s = jnp.einsum('bhqd,bhkd->bhqk', q, k, preferred_element_type=F32) * (MLA_QK ** -0.5)
    p = jax.nn.softmax(s, axis=-1).astype(v.dtype)
    return jnp.einsum('bhqk,bhkd->bhqd', p, v)


def blocked_attend(q, k, v):
    b, h, s, d = q.shape
    qb = jnp.moveaxis(q.reshape(b, h, s // Q_BLOCK, Q_BLOCK, d), 2, 0)
    ob = lax.map(lambda qi: attend(qi, k, v), qb)
    return jnp.moveaxis(ob, 0, 2).reshape(b, h, s, v.shape[-1])


def chunk_mask(inclusive):
    idx = jnp.arange(CHUNK)
    return idx[:, None] >= idx[None, :] if inclusive else idx[:, None] > idx[None, :]


def gla_chunk_scan(q, k, v, log_a, s0, inclusive):
    b_, h, t, dk = k.shape
    dv = v.shape[-1]
    n = t // CHUNK
    kc = k.reshape(b_, h, n, CHUNK, dk).astype(F32)
    vc = v.reshape(b_, h, n, CHUNK, dv).astype(F32)
    cum = jnp.cumsum(log_a.reshape(b_, h, n, CHUNK, dk).astype(F32), axis=3)
    cum_last = cum[:, :, :, -1]
    inc = jnp.einsum('bhnjd,bhnjv->bhndv', kc * jnp.exp(cum_last[:, :, :, None] - cum), vc)

    def step(s, xs):
        decay, u = xs
        return decay[..., None] * s + u, s

    s_final, s_start = lax.scan(step, s0, (jnp.moveaxis(jnp.exp(cum_last), 2, 0), jnp.moveaxis(inc, 2, 0)))
    if q is None:
        return None, s_final
    s_start = jnp.moveaxis(s_start, 0, 2)
    q_dec = q.reshape(b_, h, n, CHUNK, dk).astype(F32) * jnp.exp(cum)
    att = jnp.einsum('bhnid,bhnjd->bhnij', q_dec, kc * jnp.exp(-cum))
    att = jnp.where(chunk_mask(inclusive), att, 0.0)
    o = jnp.einsum('bhnij,bhnjv->bhniv', att, vc) + jnp.einsum('bhnid,bhndv->bhniv', q_dec, s_start)
    return o.reshape(b_, h, t, dv).astype(v.dtype), s_final


def ret_chunk_scan(q, k, v, log_g, s0, inclusive):
    b_, h, t, dk = k.shape
    dv = v.shape[-1]
    n = t // CHUNK
    idx = jnp.arange(CHUNK, dtype=F32)
    lg = log_g.astype(F32)
    kc = k.reshape(b_, h, n, CHUNK, dk).astype(F32)
    vc = v.reshape(b_, h, n, CHUNK, dv).astype(F32)
    zeta = jnp.exp((CHUNK - 1 - idx)[None, :] * lg[:, None])
    inc = jnp.einsum('bhnjd,bhnjv->bhndv', kc * zeta[None, :, None, :, None], vc)
    g_chunk = jnp.exp(CHUNK * lg)[None, :, None, None]

    def step(s, u):
        return g_chunk * s + u, s

    s_final, s_start = lax.scan(step, s0, jnp.moveaxis(inc, 2, 0))
    if q is None:
        return None, s_final
    s_start = jnp.moveaxis(s_start, 0, 2)
    mask = chunk_mask(inclusive)
    rel = jnp.where(mask, idx[:, None] - idx[None, :], 0.0)
    dmat = jnp.where(mask[None], jnp.exp(rel[None] * lg[:, None, None]), 0.0)
    xi = jnp.exp((idx + 1.0)[None, :] * lg[:, None])
    qc = q.reshape(b_, h, n, CHUNK, dk).astype(F32)
    att = jnp.einsum('bhnid,bhnjd->bhnij', qc, kc) * dmat[None, :, None]
    o = (jnp.einsum('bhnij,bhnjv->bhniv', att, vc)
         + jnp.einsum('bhnid,bhndv->bhniv', qc, s_start) * xi[None, :, None, :, None])
    return o.reshape(b_, h, t, dv).astype(v.dtype), s_final


def scan_both_directions(chunk_fn, q, k, v, dec_f, dec_b, s_f, s_b, per_token_decay):
    rev = lambda t: None if t is None else jnp.flip(t, axis=2)
    o_f, s_f = chunk_fn(q, k, v, dec_f, s_f, True)
    o_b, s_b = chunk_fn(rev(q), rev(k), rev(v), rev(dec_b) if per_token_decay else dec_b, s_b, False)
    o = None if q is None else o_f + rev(o_b)
    return o, s_f, s_b


def gla_log_decay(r, w2, b):
    return jax.nn.log_sigmoid((r @ w2 + b).astype(F32)) / GLA_GATE_NORMALIZER


def gated_head_norm(o, g, w):
    return merge_heads(rms_norm(o, w)).astype(g.dtype) * jax.nn.silu(g)


def gated_merge(ys, gs, b_gate, w_branch, w_out):
    out = None
    for n in range(N_BRANCH):
        term = jax.nn.sigmoid(gs[n] + b_gate[n]) * (ys[n] @ w_branch[n])
        out = term if out is None else out + term
    return out @ w_out


def token_mixers(a, ac, need_ctx, lat_tabs, ret_lat_tabs, ret_ctx_tabs, w_in, b_gate,
                 mla_q_norm_a, mla_w_qb, mla_kv_norm_a, mla_w_kvb, mla_q_norm, mla_k_norm,
                 gla_w_gk2, gla_b_gk, gla_o_norm, ret_decay, w_branch, w_out):
    bsz = a.shape[0]
    p = in_proj(a, w_in, CTX_SIDE + QUERY_SIDE)
    pc = in_proj(ac, w_in, CTX_SIDE + QUERY_SIDE if need_ctx else CTX_SIDE)

    k_c, v_c = mla_keys_values(pc['mla_kv'], pc['mla_kr'], mla_kv_norm_a, mla_w_kvb, mla_k_norm, None)
    k_l, v_l = mla_keys_values(p['mla_kv'], p['mla_kr'], mla_kv_norm_a, mla_w_kvb, mla_k_norm, lat_tabs)
    q_l = mla_queries(p['mla_q'], mla_q_norm_a, mla_w_qb, mla_q_norm, lat_tabs)
    y_mla = merge_heads(blocked_attend(q_l, jnp.concatenate([k_c, k_l], axis=2),
                                       jnp.concatenate([v_c, v_l], axis=2)))

    def gla_inputs(z, with_q):
        q = split_heads(z['gla_q'], GLA_HEADS) * (GLA_DK ** -0.5) if with_q else None
        k = split_heads(z['gla_k'], GLA_HEADS)
        v = split_heads(z['gla_v'], GLA_HEADS)
        la_f = split_heads(gla_log_decay(z['gla_rf'], gla_w_gk2[0], gla_b_gk[0]), GLA_HEADS)
        la_b = split_heads(gla_log_decay(z['gla_rb'], gla_w_gk2[1], gla_b_gk[1]), GLA_HEADS)
        return q, k, v, la_f, la_b

    zg = jnp.zeros((bsz, GLA_HEADS, GLA_DK, GLA_DV), F32)
    o_gc, sg_f, sg_b = scan_both_directions(gla_chunk_scan, *gla_inputs(pc, need_ctx), zg, zg, True)
    o_gl, _, _ = scan_both_directions(gla_chunk_scan, *gla_inputs(p, True), sg_f, sg_b, True)
    y_gla = gated_head_norm(o_gl, p['gla_g'], gla_o_norm)

    log_g = -jnp.exp(ret_decay.astype(F32))

    def ret_inputs(z, tabs, with_q):
        q = rotate_half(split_heads(z['ret_q'], RET_HEADS), *tabs) if with_q else None
        k = rotate_half(split_heads(z['ret_k'], RET_HEADS), *tabs) * (RET_DK ** -0.5)
        v = split_heads(z['ret_v'], RET_HEADS)
        return q, k, v

    zr = jnp.zeros((bsz, RET_HEADS, RET_DK, RET_DV), F32)
    o_rc, sr_f, sr_b = scan_both_directions(ret_chunk_scan, *ret_inputs(pc, ret_ctx_tabs, need_ctx),
                                            log_g[0], log_g[1], zr, zr, False)
    o_rl, _, _ = scan_both_directions(ret_chunk_scan, *ret_inputs(p, ret_lat_tabs, True),
                                      log_g[0], log_g[1], sr_f, sr_b, False)
    y_ret = gated_head_norm(o_rl, p['ret_g'], None)

    y = gated_merge((y_mla, y_gla, y_ret), (p['gate_mla'], p['gate_gla'], p['gate_ret']),
                    b_gate, w_branch, w_out)
    if not need_ctx:
        return y, None
    q_c = mla_queries(pc['mla_q'], mla_q_norm_a, mla_w_qb, mla_q_norm, None)
    y_c = gated_merge((merge_heads(attend(q_c, k_c, v_c)),
                       gated_head_norm(o_gc, pc['gla_g'], gla_o_norm),
                       gated_head_norm(o_rc, pc['ret_g'], None)),
                      (pc['gate_mla'], pc['gate_gla'], pc['gate_ret']), b_gate, w_branch, w_out)
    return y, y_c


def conv_ffn(a, w_in, w_dw, b_dw, w_out):
    gate = a @ w_in[:, :D_FF]
    up = a @ w_in[:, D_FF:]
    gate = lax.conv_general_dilated(gate, w_dw[:, None, :], window_strides=(1,),
                                    padding=[(CONV_W // 2, CONV_W // 2)],
                                    dimension_numbers=('NWC', 'WIO', 'NWC'),
                                    feature_group_count=D_FF) + b_dw
    return (jax.nn.gelu(gate) * up) @ w_out


def setup_inputs(seed: int = 0) -> dict:
    key = jax.random.key(seed)
    ks = iter(jax.random.split(key, 32))

    def nrm(shape, scale):
        return scale * jax.random.normal(next(ks), shape, F32)

    def gain(shape):
        return 1.0 + nrm(shape, 0.02)

    L, D = DEPTH, D_MODEL
    ret_base = jnp.log(-jnp.log1p(-(2.0 ** (-5.0 - jnp.arange(RET_HEADS, dtype=F32)))))
    return {
        'x': nrm((BATCH, SEQ, D), 1.0),
        'c': nrm((BATCH, D), 1.0),
        'ctx': nrm((BATCH, CTX_LEN, D), 1.0),
        'c_ctx': nrm((D,), 1.0),
        'w_ada': nrm((L, D, 6 * D), 0.5 * D ** -0.5),
        'b_ada': nrm((L, 6 * D), 0.02),
        'norm1_w': gain((L, D)),
        'norm2_w': gain((L, D)),
        'w_in': nrm((L, D, N_IN), D ** -0.5),
        'b_gate': nrm((L, N_BRANCH, D), 0.02),
        'mla_q_norm_a': gain((L, MLA_Q_LORA)),
        'mla_w_qb': nrm((L, MLA_Q_LORA, MLA_HEADS * MLA_QK), MLA_Q_LORA ** -0.5),
        'mla_kv_norm_a': gain((L, MLA_KV_LORA)),
        'mla_w_kvb': nrm((L, MLA_KV_LORA, MLA_HEADS * (MLA_NOPE + MLA_V)), MLA_KV_LORA ** -0.5),
        'mla_q_norm': gain((L, MLA_QK)),
        'mla_k_norm': gain((L, MLA_QK)),
        'gla_w_gk2': nrm((L, 2, GLA_GATE_RANK, GLA_HEADS * GLA_DK), GLA_GATE_RANK ** -0.5),
        'gla_b_gk': nrm((L, 2, GLA_HEADS * GLA_DK), 0.1),
        'gla_o_norm': gain((L, GLA_DV)),
        'ret_decay': ret_base + nrm((L, 2, RET_HEADS), 0.01),
        'w_branch': nrm((L, N_BRANCH, BRANCH_W, D), BRANCH_W ** -0.5),
        'w_out': nrm((L, D, D), D ** -0.5),
        'w_ffn_in': nrm((L, D, 2 * D_FF), D ** -0.5),
        'w_dw': nrm((L, CONV_W, D_FF), CONV_W ** -0.5),
        'b_dw': nrm((L, D_FF), 0.02),
        'w_ffn_out': nrm((L, D_FF, D), D_FF ** -0.5),
    }


def reference(x, c, ctx, c_ctx, w_ada, b_ada, norm1_w, norm2_w, w_in, b_gate,
              mla_q_norm_a, mla_w_qb, mla_kv_norm_a, mla_w_kvb, mla_q_norm, mla_k_norm,
              gla_w_gk2, gla_b_gk, gla_o_norm, ret_decay, w_branch, w_out,
              w_ffn_in, w_dw, b_dw, w_ffn_out):
    seq = x.shape[1]
    ctx_len = ctx.shape[1]
    rows = seq // GRID_W
    row_pos = jnp.repeat(jnp.arange(rows), GRID_W)
    col_pos = jnp.tile(jnp.arange(GRID_W), rows)
    cos_r, sin_r = rope_tables(row_pos, MLA_ROPE // 2, ROPE_THETA)
    cos_c, sin_c = rope_tables(col_pos, MLA_ROPE // 2, ROPE_THETA)
    lat_tabs = (cos_r, sin_r, cos_c, sin_c)
    ret_ctx_tabs = retention_tables(jnp.arange(ctx_len))
    ret_lat_tabs = retention_tables(ctx_len + jnp.arange(seq))
    cond = jax.nn.silu(c)
    cond_c = jax.nn.silu(c_ctx)
    h, hc = x, ctx
    for l in range(DEPTH):
        need_ctx = l < DEPTH - 1
        mod = jnp.split((cond @ w_ada[l] + b_ada[l])[:, None, :], 6, axis=-1)
        mod_c = jnp.split(cond_c @ w_ada[l] + b_ada[l], 6, axis=-1)
        a = modulate(rms_norm(h, norm1_w[l]), mod[0], mod[1])
        ac = modulate(rms_norm(hc, norm1_w[l]), mod_c[0], mod_c[1])
        y, y_c = token_mixers(a, ac, need_ctx, lat_tabs, ret_lat_tabs, ret_ctx_tabs, w_in[l], b_gate[l],
                              mla_q_norm_a[l], mla_w_qb[l], mla_kv_norm_a[l], mla_w_kvb[l],
                              mla_q_norm[l], mla_k_norm[l], gla_w_gk2[l], gla_b_gk[l], gla_o_norm[l],
                              ret_decay[l], w_branch[l], w_out[l])
        h = h + mod[2] * y
        h = h + mod[5] * conv_ffn(modulate(rms_norm(h, norm2_w[l]), mod[3], mod[4]),
                                  w_ffn_in[l], w_dw[l], b_dw[l], w_ffn_out[l])
        if need_ctx:
            hc = hc + mod_c[2] * y_c
            hc = hc + mod_c[5] * conv_ffn(modulate(rms_norm(hc, norm2_w[l]), mod_c[3], mod_c[4]),
                                          w_ffn_in[l], w_dw[l], b_dw[l], w_ffn_out[l])
    return h
```

```python
import functools
import math

import numpy as np
import jax
import jax.numpy as jnp
from jax import lax
from jax.experimental import pallas as pl
from jax.experimental.pallas import tpu as pltpu

F32 = jnp.float32
BF16 = jnp.bfloat16

D_MODEL = 1024
GRID_W = 64
N_BRANCH = 3
BRANCH_W = 512
MLA_HEADS = 8
MLA_NOPE = 64
MLA_ROPE = 32
MLA_QK = MLA_NOPE + MLA_ROPE
MLA_V = BRANCH_W // MLA_HEADS
MLA_Q_LORA = 256
MLA_KV_LORA = 128
GLA_HEADS = 4
GLA_DK = 128
GLA_DV = 128
GLA_GATE_RANK = 16
GLA_GATE_NORMALIZER = 16.0
RET_HEADS = 4
RET_DK = 128
RET_DV = 128
D_FF = 2816
CHUNK = 64
ROPE_THETA = 10000.0
RET_THETA = 10000.0
EPS = 1e-6
LANE = 128
SUBLANE = 8
HALO = 16
VMEM_LIMIT = 56 * 1024 * 1024

_IN_WIDTHS = (('mla_q', 256), ('mla_kv', 128), ('mla_kr', 32), ('gla_q', 512), ('gla_k', 512),
              ('gla_v', 512), ('gla_g', 512), ('gla_rf', 16), ('gla_rb', 16), ('ret_q', 512),
              ('ret_k', 512), ('ret_v', 512), ('ret_g', 512), ('gate_mla', 1024),
              ('gate_gla', 1024), ('gate_ret', 1024))
_IN_OFF = {}
_o = 0
for _n, _w in _IN_WIDTHS:
    _IN_OFF[_n] = _o
    _o += _w
N_IN = _o
LAT_W = 5 * LANE
MIX_W = 8 * 512
GATE_W = 3 * D_MODEL


def _cparams(sem):
    return pltpu.CompilerParams(dimension_semantics=sem, vmem_limit_bytes=VMEM_LIMIT)


def _pick(n, pref, mult=SUBLANE):
    if n <= pref:
        return n
    for t in range(pref - pref % mult, 0, -mult):
        if n % t == 0:
            return t
    return n


def _rope_partner(nrot):
    half = nrot // 2
    q = half // 2
    idx = np.zeros(nrot, np.int32)
    sgn = np.zeros(nrot, np.float32)
    for base in (0, half):
        for i in range(q):
            idx[base + i] = base + i + q
            sgn[base + i] = -1.0
            idx[base + q + i] = base + i
            sgn[base + q + i] = 1.0
    return idx, sgn


def _ada_kernel(c_ref, w_ref, b_ref, o_ref):
    c = c_ref[...]
    cs = c * jax.nn.sigmoid(c)
    o_ref[0] = jnp.dot(cs, w_ref[0], preferred_element_type=F32,
                       precision=lax.Precision.HIGHEST) + b_ref[0]


def _ada_mod(cc, w_ada, b_ada):
    nl, d, n = w_ada.shape
    tn = _pick(n, 1536, LANE)
    return pl.pallas_call(
        _ada_kernel,
        out_shape=jax.ShapeDtypeStruct((nl, cc.shape[0], n), F32),
        grid=(nl, n // tn),
        in_specs=[pl.BlockSpec((cc.shape[0], d), lambda l, j: (0, 0)),
                  pl.BlockSpec((1, d, tn), lambda l, j: (l, 0, j)),
                  pl.BlockSpec((1, 1, tn), lambda l, j: (l, 0, j))],
        out_specs=pl.BlockSpec((1, cc.shape[0], tn), lambda l, j: (l, 0, j)),
        compiler_params=_cparams(("arbitrary", "arbitrary")),
        name="ada_mod",
    )(cc, w_ada, b_ada.reshape(nl, 1, n))


def _nmm_kernel(x_ref, sh_ref, sc_ref, nw_ref, w_ref, o_ref, a_scr):
    @pl.when(pl.program_id(1) == 0)
    def _():
        x = x_ref[...]
        y = x * lax.rsqrt(jnp.mean(x * x, axis=-1, keepdims=True) + EPS) * nw_ref[...]
        a_scr[...] = (y * (1.0 + sc_ref[0]) + sh_ref[0]).astype(BF16)

    o_ref[...] = jnp.dot(a_scr[...], w_ref[...], preferred_element_type=F32).astype(o_ref.dtype)


def _norm_mod_matmul(x, shift, scale, nw, w, out_dtype, tm, tn, name):
    r, d = x.shape
    g = shift.shape[0]
    rpg = r // g
    n = w.shape[1]
    tm = _pick(rpg, tm)
    tn = _pick(n, tn, LANE)
    return pl.pallas_call(
        _nmm_kernel,
        out_shape=jax.ShapeDtypeStruct((r, n), out_dtype),
        grid=(r // tm, n // tn),
        in_specs=[pl.BlockSpec((tm, d), lambda i, j: (i, 0)),
                  pl.BlockSpec((1, 1, d), lambda i, j: (i * tm // rpg, 0, 0)),
                  pl.BlockSpec((1, 1, d), lambda i, j: (i * tm // rpg, 0, 0)),
                  pl.BlockSpec((1, d), lambda i, j: (0, 0)),
                  pl.BlockSpec((d, tn), lambda i, j: (0, j))],
        out_specs=pl.BlockSpec((tm, tn), lambda i, j: (i, j)),
        scratch_shapes=[pltpu.VMEM((tm, d), BF16)],
        compiler_params=_cparams(("parallel", "arbitrary")),
        name=name,
    )(x, shift, scale, nw, w)


def _rms(x, n):
    return lax.rsqrt(jnp.sum(x * x, axis=-1, keepdims=True) * (1.0 / n) + EPS)


def _mla_prep_kernel(cq_ref, ckv_ref, kr_ref, krp_ref, cos_ref, sin_ref, qna_ref, kvna_ref,
                     wq_ref, wk_ref, wv_ref, qn_ref, qnp_ref, kn_ref, knp_ref,
                     q_out, k_out, v_out):
    cos = cos_ref[...]
    sin = sin_ref[...]
    cq = cq_ref[...]
    cqn = (cq * _rms(cq, MLA_Q_LORA) * qna_ref[...]).astype(BF16)
    qx = jnp.dot(cqn, wq_ref[...], preferred_element_type=F32)
    ckv = ckv_ref[...]
    ckvn = (ckv * _rms(ckv, MLA_KV_LORA) * kvna_ref[...]).astype(BF16)
    kx = jnp.dot(ckvn, wk_ref[...], preferred_element_type=F32)
    vx = jnp.dot(ckvn, wv_ref[...], preferred_element_type=F32)
    lane = lax.broadcasted_iota(jnp.int32, kr_ref.shape, 1)
    krm = jnp.where(lane < MLA_QK, kr_ref[...], 0.0)
    krp = krp_ref[...]
    ssq_kr = jnp.sum(krm * krm, axis=-1, keepdims=True)
    qc = qn_ref[...] * cos
    qs = qnp_ref[...] * sin
    kc = kn_ref[...] * cos
    ks = knp_ref[...] * sin
    hw = MLA_HEADS * LANE
    for h in range(MLA_HEADS):
        sl = slice(h * LANE, (h + 1) * LANE)
        xa = qx[:, sl]
        xb = qx[:, hw + h * LANE: hw + (h + 1) * LANE]
        r = _rms(xa, MLA_QK) * (MLA_QK ** -0.5)
        q_out[0, h] = ((xa * qc + xb * qs) * r).astype(BF16)
        kn = kx[:, sl]
        rk = lax.rsqrt((jnp.sum(kn * kn, axis=-1, keepdims=True) + ssq_kr) * (1.0 / MLA_QK) + EPS)
        k_out[0, h] = (((kn + krm) * kc + krp * ks) * rk).astype(BF16)
        v_out[0, h] = vx[:, sl].astype(BF16)


def _mla_prep(lat, cos_t, sin_t, qna, kvna, wq2, wk, wv, qn, qnp, kn, knp, bsz, seq):
    tm = _pick(seq, 512)
    nb = seq // tm
    hw = MLA_HEADS * LANE
    row = lambda b, s: b * nb + s
    const = lambda b, s: (0, 0)
    oshape = jax.ShapeDtypeStruct((bsz, MLA_HEADS, seq, LANE), BF16)
    ospec = pl.BlockSpec((1, MLA_HEADS, tm, LANE), lambda b, s: (b, 0, s, 0))
    return pl.pallas_call(
        _mla_prep_kernel,
        out_shape=(oshape, oshape, oshape),
        grid=(bsz, nb),
        in_specs=[pl.BlockSpec((tm, 2 * LANE), lambda b, s: (row(b, s), 0)),
                  pl.BlockSpec((tm, LANE), lambda b, s: (row(b, s), 2)),
                  pl.BlockSpec((tm, LANE), lambda b, s: (row(b, s), 3)),
                  pl.BlockSpec((tm, LANE), lambda b, s: (row(b, s), 4)),
                  pl.BlockSpec((tm, LANE), lambda b, s: (s, 0)),
                  pl.BlockSpec((tm, LANE), lambda b, s: (s, 0)),
                  pl.BlockSpec((1, MLA_Q_LORA), const),
                  pl.BlockSpec((1, MLA_KV_LORA), const),
                  pl.BlockSpec((MLA_Q_LORA, 2 * hw), const),
                  pl.BlockSpec((MLA_KV_LORA, hw), const),
                  pl.BlockSpec((MLA_KV_LORA, hw), const),
                  pl.BlockSpec((1, LANE), const), pl.BlockSpec((1, LANE), const),
                  pl.BlockSpec((1, LANE), const), pl.BlockSpec((1, LANE), const)],
        out_specs=(ospec, ospec, ospec),
        compiler_params=_cparams(("parallel", "arbitrary")),
        name="mla_prep",
    )(lat, lat, lat, lat, cos_t, sin_t, qna, kvna, wq2, wk, wv, qn, qnp, kn, knp)


def _attn_kernel(q_ref, k_ref, v_ref, o_ref, *, tk, nk):
    tq = q_ref.shape[2]
    out = None
    for hh in range(2):
        q = q_ref[0, hh]

        def body(c, carry, hh=hh, q=q):
            m, l, acc = carry
            off = pl.multiple_of(c * tk, tk)
            ks = k_ref[0, hh, pl.ds(off, tk), :]
            vs = v_ref[0, hh, pl.ds(off, tk), :]
            s = lax.dot_general(q, ks, (((1,), (1,)), ((), ())), preferred_element_type=F32)
            m_new = jnp.maximum(m, jnp.max(s, axis=-1, keepdims=True))
            p = jnp.exp(s - m_new)
            a = jnp.exp(m - m_new)
            l = a * l + jnp.sum(p, axis=-1, keepdims=True)
            acc = a * acc + jnp.dot(p.astype(BF16), vs, preferred_element_type=F32)
            return m_new, l, acc

        init = (jnp.full((tq, 1), -jnp.inf, F32), jnp.zeros((tq, 1), F32),
                jnp.zeros((tq, LANE), F32))
        m, l, acc = lax.fori_loop(0, nk, body, init)
        o = acc * (1.0 / l)
        out = o if out is None else out + o
    o_ref[0] = out.astype(o_ref.dtype)


def _attention(q, k, v):
    bsz, nh, sq, _ = q.shape
    sk = k.shape[2]
    tq = _pick(sq, 512)
    tk = next(t for t in (768, 512, 256, 128, sk) if sk % t == 0)
    kern = functools.partial(_attn_kernel, tk=tk, nk=sk // tk)
    return pl.pallas_call(
        kern,
        out_shape=jax.ShapeDtypeStruct((bsz, sq, nh * MLA_V), BF16),
        grid=(bsz, nh // 2, sq // tq),
        in_specs=[pl.BlockSpec((1, 2, tq, LANE), lambda b, h, i: (b, h, i, 0)),
                  pl.BlockSpec((1, 2, sk, LANE), lambda b, h, i: (b, h, 0, 0)),
                  pl.BlockSpec((1, 2, sk, LANE), lambda b, h, i: (b, h, 0, 0))],
        out_specs=pl.BlockSpec((1, tq, LANE), lambda b, h, i: (b, i, h)),
        compiler_params=_cparams(("parallel", "parallel", "arbitrary")),
        name="mla_attention",
    )(q, k, v)


def _dot_nt(a, b):
    return lax.dot_general(a, b, (((1,), (1,)), ((), ())), preferred_element_type=F32)


def _dot_tn(a, b):
    return lax.dot_general(a, b, (((0,), (0,)), ((), ())), preferred_element_type=F32)


def _scan_kernel(*refs, kind, reverse, nchunk, epilogue):
    it = iter(refs)
    q_ref, k_ref, v_ref = next(it), next(it), next(it)
    if kind == 'gla':
        r_ref, w2_ref, bg_ref = next(it), next(it), next(it)
    else:
        cos_ref, sin_ref, rd_ref = next(it), next(it), next(it)
    s0_ref = next(it)
    if epilogue:
        oo_ref, g_ref, nw_ref = next(it), next(it), next(it)
    o_ref, sf_ref, st_scr = next(it), next(it), next(it)

    step = pl.program_id(1)

    @pl.when(step == 0)
    def _():
        st_scr[...] = s0_ref[0]

    ii = lax.broadcasted_iota(jnp.int32, (CHUNK, CHUNK), 0)
    jj = lax.broadcasted_iota(jnp.int32, (CHUNK, CHUNK), 1)
    if reverse:
        tri = (jj >= ii).astype(F32)
        amask = jj > ii
    else:
        tri = (ii >= jj).astype(F32)
        amask = ii >= jj

    if kind == 'gla':
        pre = jnp.dot(r_ref[...], w2_ref[...], preferred_element_type=F32,
                      precision=lax.Precision.HIGHEST) + bg_ref[...]
        la_all = jax.nn.log_sigmoid(pre) * (1.0 / GLA_GATE_NORMALIZER)
    else:
        ridx = lax.broadcasted_iota(jnp.int32, (CHUNK, LANE), 0).astype(F32)
        dd = (jj - ii if reverse else ii - jj).astype(F32)

    order = range(nchunk - 1, -1, -1) if reverse else range(nchunk)
    for c in order:
        rows = slice(c * CHUNK, (c + 1) * CHUNK)
        for h in range(4):
            cols = slice(h * LANE, (h + 1) * LANE)
            q = q_ref[rows, cols]
            k = k_ref[rows, cols]
            vb = v_ref[rows, cols].astype(BF16)
            st = st_scr[h]
            if kind == 'gla':
                la = la_all[rows, cols]
                cum = jnp.dot(tri, la, preferred_element_type=F32, precision=lax.Precision.HIGHEST)
                tot = cum[0:1] if reverse else cum[CHUNK - 1:CHUNK]
                qd = (q * (GLA_DK ** -0.5) * jnp.exp(cum)).astype(BF16)
                kd = (k * jnp.exp(-cum)).astype(BF16)
                kz = (k * jnp.exp(tot - cum)).astype(BF16)
                att = jnp.where(amask, _dot_nt(qd, kd), 0.0)
                o = (jnp.dot(att.astype(BF16), vb, preferred_element_type=F32)
                     + _dot_nt(qd, st.astype(BF16)))
                st_new = st * jnp.exp(tot) + _dot_tn(vb, kz)
            else:
                lg = -jnp.exp(rd_ref[h:h + 1, :])
                cs = cos_ref[rows, :]
                sn = sin_ref[rows, :]
                qr = (q * cs + pltpu.roll(q, LANE // 2, 1) * sn).astype(BF16)
                kr = (k * cs + pltpu.roll(k, LANE // 2, 1) * sn) * (RET_DK ** -0.5)
                if reverse:
                    zeta = jnp.exp(ridx * lg)
                    xi = jnp.exp((CHUNK - ridx) * lg)
                else:
                    zeta = jnp.exp((CHUNK - 1 - ridx) * lg)
                    xi = jnp.exp((ridx + 1.0) * lg)
                dmat = jnp.where(amask, jnp.exp(jnp.where(amask, dd, 0.0) * lg[:, :CHUNK]), 0.0)
                att = _dot_nt(qr, kr.astype(BF16)) * dmat
                o = (jnp.dot(att.astype(BF16), vb, preferred_element_type=F32)
                     + _dot_nt(qr, st.astype(BF16)) * xi)
                st_new = st * jnp.exp(CHUNK * lg) + _dot_tn(vb, (kr * zeta).astype(BF16))
            st_scr[h] = st_new
            if epilogue:
                osum = o + oo_ref[rows, cols]
                g = g_ref[rows, cols]
                y = osum * _rms(osum, LANE) * nw_ref[...]
                o_ref[rows, cols] = (y * (g * jax.nn.sigmoid(g))).astype(o_ref.dtype)
            else:
                o_ref[rows, cols] = o

    @pl.when(step == pl.num_programs(1) - 1)
    def _():
        sf_ref[0] = st_scr[...]


def _scan_pass(kind, reverse, mix, col0, bsz, seq, s0, extra, epi):
    blk = _pick(seq, 256, CHUNK)
    nb = seq // blk
    pos = (lambda s: nb - 1 - s) if reverse else (lambda s: s)
    row = lambda b, s: b * nb + pos(s)
    const = lambda b, s: (0, 0)
    in_specs = [pl.BlockSpec((blk, 512), lambda b, s, c=col0 + j: (row(b, s), c)) for j in range(3)]
    args = [mix, mix, mix]
    if kind == 'gla':
        lat, w2p, bg = extra
        in_specs += [pl.BlockSpec((blk, LANE), lambda b, s: (row(b, s), 3)),
                     pl.BlockSpec((LANE, 512), const), pl.BlockSpec((1, 512), const)]
        args += [lat, w2p, bg]
    else:
        cos2, sin2, rd = extra
        in_specs += [pl.BlockSpec((blk, LANE), lambda b, s: (pos(s), 0)),
                     pl.BlockSpec((blk, LANE), lambda b, s: (pos(s), 0)),
                     pl.BlockSpec((SUBLANE, LANE), const)]
        args += [cos2, sin2, rd]
    in_specs.append(pl.BlockSpec((1, 4, LANE, LANE), lambda b, s: (b, 0, 0, 0)))
    args.append(s0)
    if epi is not None:
        oo, gcol, nw = epi
        in_specs += [pl.BlockSpec((blk, 512), lambda b, s: (row(b, s), 0)),
                     pl.BlockSpec((blk, 512), lambda b, s: (row(b, s), gcol)),
                     pl.BlockSpec((1, LANE), const)]
        args += [oo, mix, nw]
    r = bsz * seq
    kern = functools.partial(_scan_kernel, kind=kind, reverse=reverse, nchunk=blk // CHUNK,
                             epilogue=epi is not None)
    return pl.pallas_call(
        kern,
        out_shape=(jax.ShapeDtypeStruct((r, 512), BF16 if epi is not None else F32),
                   jax.ShapeDtypeStruct((bsz, 4, LANE, LANE), F32)),
        grid=(bsz, nb),
        in_specs=in_specs,
        out_specs=(pl.BlockSpec((blk, 512), lambda b, s: (row(b, s), 0)),
                   pl.BlockSpec((1, 4, LANE, LANE), lambda b, s: (b, 0, 0, 0))),
        scratch_shapes=[pltpu.VMEM((4, LANE, LANE), F32)],
        compiler_params=_cparams(("parallel", "arbitrary")),
        name=f"{kind}_scan_{'bwd' if reverse else 'fwd'}",
    )(*args)


def _merge_kernel(y0_ref, y1_ref, y2_ref, g_ref, bg_ref, wb_ref, wo_ref, h_ref, gt_ref, o_ref):
    u = None
    for n, y_ref in enumerate((y0_ref, y1_ref, y2_ref)):
        cols = slice(n * D_MODEL, (n + 1) * D_MODEL)
        t = jnp.dot(y_ref[...], wb_ref[n], preferred_element_type=F32)
        t = jax.nn.sigmoid(g_ref[:, cols] + bg_ref[:, cols]) * t
        u = t if u is None else u + t
    out = jnp.dot(u.astype(BF16), wo_ref[...], preferred_element_type=F32)
    o_ref[...] = h_ref[...] + gt_ref[0] * out


def _merge(ys, gates, bg, wb, wo, h, gt):
    r, d = h.shape
    g = gt.shape[0]
    rpg = r // g
    tm = _pick(rpg, 512)
    yspec = pl.BlockSpec((tm, BRANCH_W), lambda i: (i, 0))
    return pl.pallas_call(
        _merge_kernel,
        out_shape=jax.ShapeDtypeStruct((r, d), F32),
        grid=(r // tm,),
        in_specs=[yspec, yspec, yspec,
                  pl.BlockSpec((tm, GATE_W), lambda i: (i, 0)),
                  pl.BlockSpec((1, GATE_W), lambda i: (0, 0)),
                  pl.BlockSpec((N_BRANCH, BRANCH_W, d), lambda i: (0, 0, 0)),
                  pl.BlockSpec((d, d), lambda i: (0, 0)),
                  pl.BlockSpec((tm, d), lambda i: (i, 0)),
                  pl.BlockSpec((1, 1, d), lambda i: (i * tm // rpg, 0, 0))],
        out_specs=pl.BlockSpec((tm, d), lambda i: (i, 0)),
        compiler_params=_cparams(("parallel",)),
        name="gated_merge",
    )(ys[0], ys[1], ys[2], gates, bg, wb, wo, h, gt)


def _ffn_kernel(hp_ref, h_ref, hn_ref, sh_ref, sc_ref, gt_ref, nw_ref, wg_ref, wu_ref, wdw_ref,
                bdw_ref, wo_ref, o_ref, a_scr, acc_scr, *, tm, seq):
    i = pl.program_id(0)
    f = pl.program_id(1)

    def norm_mod(x):
        y = x * lax.rsqrt(jnp.mean(x * x, axis=-1, keepdims=True) + EPS) * nw_ref[...]
        return (y * (1.0 + sc_ref[0]) + sh_ref[0]).astype(BF16)

    @pl.when(f == 0)
    def _():
        a_scr[0:HALO, :] = norm_mod(hp_ref[...])
        a_scr[HALO:HALO + tm, :] = norm_mod(h_ref[...])
        a_scr[HALO + tm:, :] = norm_mod(hn_ref[...])
        acc_scr[...] = jnp.zeros_like(acc_scr)

    gate = jnp.dot(a_scr[...], wg_ref[...], preferred_element_type=F32)
    up = jnp.dot(a_scr[HALO:HALO + tm, :], wu_ref[...], preferred_element_type=F32)
    row0 = i * tm
    prev_ok = (row0 % seq != 0).astype(F32)
    next_ok = ((row0 + tm) % seq != 0).astype(F32)
    rid = lax.broadcasted_iota(jnp.int32, (tm + 2 * HALO, 1), 0)
    rscale = jnp.where(rid == HALO - 1, prev_ok, jnp.where(rid == HALO + tm, next_ok, 1.0))
    gate = gate * rscale
    g_prev = pltpu.roll(gate, 1, 0)[HALO:HALO + tm]
    g_next = pltpu.roll(gate, tm + 2 * HALO - 1, 0)[HALO:HALO + tm]
    wdw = wdw_ref[...]
    conv = (g_prev * wdw[0:1] + gate[HALO:HALO + tm] * wdw[1:2] + g_next * wdw[2:3]
            + bdw_ref[...])
    act = jax.nn.gelu(conv, approximate=True) * up
    acc_scr[...] += jnp.dot(act.astype(BF16), wo_ref[...], preferred_element_type=F32)

    @pl.when(f == pl.num_programs(1) - 1)
    def _():
        o_ref[...] = h_ref[...] + gt_ref[0] * acc_scr[...]


def _ffn(h, shift, scale, gt, nw, wg, wu, wdw, bdw, wo, seq):
    r, d = h.shape
    g = shift.shape[0]
    rpg = r // g
    dff = wg.shape[1]
    tm = _pick(seq, 512, HALO)
    tf = _pick(dff, 1408, LANE)
    nsub = tm // HALO
    nrb = r // HALO
    gidx = lambda i, f: (i * tm // rpg, 0, 0)
    kern = functools.partial(_ffn_kernel, tm=tm, seq=seq)
    return pl.pallas_call(
        kern,
        out_shape=jax.ShapeDtypeStruct((r, d), F32),
        grid=(r // tm, dff // tf),
        in_specs=[pl.BlockSpec((HALO, d), lambda i, f: (jnp.maximum(i * nsub - 1, 0), 0)),
                  pl.BlockSpec((tm, d), lambda i, f: (i, 0)),
                  pl.BlockSpec((HALO, d), lambda i, f: (jnp.minimum((i + 1) * nsub, nrb - 1), 0)),
                  pl.BlockSpec((1, 1, d), gidx), pl.BlockSpec((1, 1, d), gidx),
                  pl.BlockSpec((1, 1, d), gidx),
                  pl.BlockSpec((1, d), lambda i, f: (0, 0)),
                  pl.BlockSpec((d, tf), lambda i, f: (0, f)),
                  pl.BlockSpec((d, tf), lambda i, f: (0, f)),
                  pl.BlockSpec((3, tf), lambda i, f: (0, f)),
                  pl.BlockSpec((1, tf), lambda i, f: (0, f)),
                  pl.BlockSpec((tf, d), lambda i, f: (f, 0))],
        out_specs=pl.BlockSpec((tm, d), lambda i, f: (i, 0)),
        scratch_shapes=[pltpu.VMEM((tm + 2 * HALO, d), BF16), pltpu.VMEM((tm, d), F32)],
        compiler_params=_cparams(("parallel", "arbitrary")),
        name="conv_ffn",
    )(h, h, h, shift, scale, gt, nw, wg, wu, wdw, bdw, wo)


def _in_proj_layout():
    idx, sgn = [], []

    def take(name, width, off=0):
        idx.extend(range(_IN_OFF[name] + off, _IN_OFF[name] + off + width))
        sgn.extend([1.0] * width)

    def zeros(width):
        idx.extend([0] * width)
        sgn.extend([0.0] * width)

    take('mla_q', 256)
    take('mla_kv', 128)
    zeros(64); take('mla_kr', 32); take('gla_rf', 16); take('gla_rb', 16)
    zeros(64)
    pidx, psgn = _rope_partner(MLA_ROPE)
    idx.extend((_IN_OFF['mla_kr'] + pidx).tolist()); sgn.extend(psgn.tolist())
    zeros(32)
    for name in ('gla_q', 'gla_k', 'gla_v', 'gla_g', 'ret_q', 'ret_k', 'ret_v', 'ret_g'):
        take(name, 512)
    for name in ('gate_mla', 'gate_gla', 'gate_ret'):
        take(name, 1024)
    return np.asarray(idx, np.int32), np.asarray(sgn, np.float32)


def _mla_weight_layout():
    pidx, psgn = _rope_partner(MLA_ROPE)
    qa_idx, qa_sgn, qb_idx, qb_sgn = [], [], [], []
    k_idx, k_sgn, v_idx, v_sgn = [], [], [], []
    for h in range(MLA_HEADS):
        base = h * MLA_QK
        qa_idx += list(range(base, base + MLA_QK)) + [0] * 32
        qa_sgn += [1.0] * MLA_QK + [0.0] * 32
        qb_idx += [0] * 64 + (base + MLA_NOPE + pidx).tolist() + [0] * 32
        qb_sgn += [0.0] * 64 + psgn.tolist() + [0.0] * 32
        kb = h * (MLA_NOPE + MLA_V)
        k_idx += list(range(kb, kb + MLA_NOPE)) + [0] * 64
        k_sgn += [1.0] * 64 + [0.0] * 64
        vcols = list(range(kb + MLA_NOPE, kb + MLA_NOPE + MLA_V))
        if h % 2 == 0:
            v_idx += vcols + [0] * 64
            v_sgn += [1.0] * 64 + [0.0] * 64
        else:
            v_idx += [0] * 64 + vcols
            v_sgn += [0.0] * 64 + [1.0] * 64
    f = lambda a, t: np.asarray(a, t)
    return (f(qa_idx + qb_idx, np.int32), f(qa_sgn + qb_sgn, np.float32),
            f(k_idx, np.int32), f(k_sgn, np.float32), f(v_idx, np.int32), f(v_sgn, np.float32))


def _slot_vec(w):
    pidx, _ = _rope_partner(MLA_ROPE)
    z32 = jnp.zeros((32,), F32)
    a = jnp.concatenate([w, z32])
    p = jnp.concatenate([jnp.zeros((64,), F32), w[MLA_NOPE + pidx], z32])
    return a.reshape(1, LANE), p.reshape(1, LANE)


def _mla_tables(seq, rope):
    ones = jnp.ones((seq, 64), F32)
    if not rope:
        return jnp.ones((seq, LANE), F32), jnp.zeros((seq, LANE), F32)
    pos = jnp.arange(seq)
    dim = MLA_ROPE // 2
    inv = ROPE_THETA ** (-jnp.arange(dim // 2, dtype=F32) * 2.0 / dim)
    ar = (pos // GRID_W).astype(F32)[:, None] * inv[None, :]
    ac = (pos % GRID_W).astype(F32)[:, None] * inv[None, :]
    cos = jnp.concatenate([ones, jnp.cos(ar), jnp.cos(ar), jnp.cos(ac), jnp.cos(ac),
                           jnp.ones((seq, 32), F32)], axis=1)
    sin = jnp.concatenate([0.0 * ones, jnp.sin(ar), jnp.sin(ar), jnp.sin(ac), jnp.sin(ac),
                           jnp.zeros((seq, 32), F32)], axis=1)
    return cos, sin


def _ret_tables(start, seq):
    inv = 1.0 / (RET_THETA ** jnp.linspace(0.0, 1.0, RET_DK // 2, dtype=F32))
    ang = (start + jnp.arange(seq)).astype(F32)[:, None] * inv[None, :]
    cos, sin = jnp.cos(ang), jnp.sin(ang)
    return jnp.concatenate([cos, cos], axis=1), jnp.concatenate([-sin, sin], axis=1)


def _mixers(lat, mix, bsz, seq, lw, mla_tabs, ret_tabs, states):
    q, k, v = _mla_prep(lat, mla_tabs[0], mla_tabs[1], lw['qna'], lw['kvna'], lw['wq2'], lw['wk'],
                        lw['wv'], lw['qn'], lw['qnp'], lw['kn'], lw['knp'], bsz, seq)
    sg_f, sg_b, sr_f, sr_b = states
    ob, sg_b2 = _scan_pass('gla', True, mix, 0, bsz, seq, sg_b, (lat, lw['w2b'], lw['bgb']), None)
    y_gla, sg_f2 = _scan_pass('gla', False, mix, 0, bsz, seq, sg_f, (lat, lw['w2f'], lw['bgf']),
                              (ob, 3, lw['gla_on']))
    rb, sr_b2 = _scan_pass('ret', True, mix, 4, bsz, seq, sr_b,
                           (ret_tabs[0], ret_tabs[1], lw['rd_b']), None)
    y_ret, sr_f2 = _scan_pass('ret', False, mix, 4, bsz, seq, sr_f,
                              (ret_tabs[0], ret_tabs[1], lw['rd_f']), (rb, 7, lw['ones']))
    return (q, k, v), y_gla, y_ret, (sg_f2, sg_b2, sr_f2, sr_b2)


def kernel(x, c, ctx, c_ctx, w_ada, b_ada, norm1_w, norm2_w, w_in, b_gate, mla_q_norm_a, mla_w_qb,
           mla_kv_norm_a, mla_w_kvb, mla_q_norm, mla_k_norm, gla_w_gk2, gla_b_gk, gla_o_norm,
           ret_decay, w_branch, w_out, w_ffn_in, w_dw, b_dw, w_ffn_out):
    bsz, seq, d = x.shape
    clen = ctx.shape[1]
    depth = w_ada.shape[0]
    r_lat, r_ctx = bsz * seq, bsz * clen

    npad = -(bsz + 1) % SUBLANE
    cc = jnp.concatenate([c, c_ctx[None, :], jnp.zeros((npad, d), F32)], axis=0)
    mod_all = _ada_mod(cc, w_ada, b_ada)

    in_idx, in_sgn = _in_proj_layout()
    qidx, qsgn, kidx, ksgn, vidx, vsgn = _mla_weight_layout()
    mla_lat_tabs = _mla_tables(seq, True)
    mla_ctx_tabs = _mla_tables(clen, False)
    ret_lat_tabs = _ret_tables(clen, seq)
    ret_ctx_tabs = _ret_tables(0, clen)
    zstate = jnp.zeros((bsz, 4, LANE, LANE), F32)
    ones128 = jnp.ones((1, LANE), F32)

    h = x.reshape(r_lat, d)
    hc = ctx.reshape(r_ctx, d)
    for l in range(depth):
        need_ctx = l < depth - 1
        mods = mod_all[l].reshape(-1, 6, d)
        ml = [mods[:bsz, j].reshape(bsz, 1, d) for j in range(6)]
        mc = [mods[bsz:bsz + 1, j].reshape(1, 1, d) for j in range(6)]

        w_p = (w_in[l][:, in_idx] * in_sgn[None, :]).astype(BF16)
        w_lat, w_mix, w_gate = w_p[:, :LAT_W], w_p[:, LAT_W:LAT_W + MIX_W], w_p[:, LAT_W + MIX_W:]
        qn, qnp = _slot_vec(mla_q_norm[l])
        kn, knp = _slot_vec(mla_k_norm[l])
        w2 = gla_w_gk2[l]
        zr = jnp.zeros((LANE - 2 * GLA_GATE_RANK, 4 * GLA_DK), F32)
        z16 = jnp.zeros((GLA_GATE_RANK, 4 * GLA_DK), F32)
        lw = dict(
            qna=mla_q_norm_a[l].reshape(1, -1), kvna=mla_kv_norm_a[l].reshape(1, -1),
            wq2=(mla_w_qb[l][:, qidx] * qsgn[None, :]).astype(BF16),
            wk=(mla_w_kvb[l][:, kidx] * ksgn[None, :]).astype(BF16),
            wv=(mla_w_kvb[l][:, vidx] * vsgn[None, :]).astype(BF16),
            qn=qn, qnp=qnp, kn=kn, knp=knp,
            w2f=jnp.concatenate([zr, w2[0], z16], axis=0),
            w2b=jnp.concatenate([zr, z16, w2[1]], axis=0),
            bgf=gla_b_gk[l, 0].reshape(1, -1), bgb=gla_b_gk[l, 1].reshape(1, -1),
            gla_on=gla_o_norm[l].reshape(1, LANE), ones=ones128,
            rd_f=jnp.broadcast_to(jnp.pad(ret_decay[l, 0], (0, 4))[:, None], (SUBLANE, LANE)),
            rd_b=jnp.broadcast_to(jnp.pad(ret_decay[l, 1], (0, 4))[:, None], (SUBLANE, LANE)),
        )
        nw1 = norm1_w[l].reshape(1, d)
        nw2 = norm2_w[l].reshape(1, d)
        bg = b_gate[l].reshape(1, GATE_W)
        wb = w_branch[l].astype(BF16)
        wo = w_out[l].astype(BF16)
        wg = w_ffn_in[l][:, :D_FF].astype(BF16)
        wu = w_ffn_in[l][:, D_FF:].astype(BF16)
        wfo = w_ffn_out[l].astype(BF16)
        bdw = b_dw[l].reshape(1, D_FF)

        def project(hh, m, name):
            lat = _norm_mod_matmul(hh, m[0], m[1], nw1, w_lat, F32, 1024, LAT_W, name + "_lat")
            mix = _norm_mod_matmul(hh, m[0], m[1], nw1, w_mix, F32, 1024, 1024, name + "_mix")
            return lat, mix

        lat_c, mix_c = project(hc, mc, "in_proj_ctx")
        (q_c, k_c, v_c), yg_c, yr_c, states = _mixers(
            lat_c, mix_c, bsz, clen, lw, mla_ctx_tabs, ret_ctx_tabs, (zstate,) * 4)
        lat_l, mix_l = project(h, ml, "in_proj")
        (q_l, k_l, v_l), yg_l, yr_l, _ = _mixers(
            lat_l, mix_l, bsz, seq, lw, mla_lat_tabs, ret_lat_tabs, states)
        y_mla = _attention(q_l, jnp.concatenate([k_c, k_l], axis=2),
                           jnp.concatenate([v_c, v_l], axis=2)).reshape(r_lat, BRANCH_W)
        gates_l = _norm_mod_matmul(h, ml[0], ml[1], nw1, w_gate, F32, 1024, 1024, "in_proj_gate")
        h = _merge((y_mla, yg_l, yr_l), gates_l, bg, wb, wo, h, ml[2])
        h = _ffn(h, ml[3], ml[4], ml[5], nw2, wg, wu, w_dw[l], bdw, wfo, seq)
        if need_ctx:
            ym_c = _attention(q_c, k_c, v_c).reshape(r_ctx, BRANCH_W)
            gates_c = _norm_mod_matmul(hc, mc[0], mc[1], nw1, w_gate, F32, 1024, 1024,
                                       "in_proj_gate_ctx")
            hc = _merge((ym_c, yg_c, yr_c), gates_c, bg, wb, wo, hc, mc[2])
            hc = _ffn(hc, mc[3], mc[4], mc[5], nw2, wg, wu, w_dw[l], bdw, wfo, clen)
    return h.reshape(bsz, seq, d)
```

```python
import functools
import math

import numpy as np
import jax
import jax.numpy as jnp
from jax import lax
from jax.experimental import pallas as pl
from jax.experimental.pallas import tpu as pltpu

F32 = jnp.float32
BF16 = jnp.bfloat16

D_MODEL = 1024
GRID_W = 64
N_BRANCH = 3
BRANCH_W = 512
MLA_HEADS = 8
MLA_NOPE = 64
MLA_ROPE = 32
MLA_QK = MLA_NOPE + MLA_ROPE
MLA_V = BRANCH_W // MLA_HEADS
MLA_Q_LORA = 256
MLA_KV_LORA = 128
GLA_HEADS = 4
GLA_DK = 128
GLA_DV = 128
GLA_GATE_RANK = 16
GLA_GATE_NORMALIZER = 16.0
RET_HEADS = 4
RET_DK = 128
RET_DV = 128
D_FF = 2816
CHUNK = 64
ROPE_THETA = 10000.0
RET_THETA = 10000.0
EPS = 1e-6
LOG2E = math.log2(math.e)
LANE = 128
SUBLANE = 8
HALO = 16
VMEM_LIMIT = 56 * 1024 * 1024

_IN_WIDTHS = (('mla_q', 256), ('mla_kv', 128), ('mla_kr', 32), ('gla_q', 512), ('gla_k', 512),
              ('gla_v', 512), ('gla_g', 512), ('gla_rf', 16), ('gla_rb', 16), ('ret_q', 512),
              ('ret_k', 512), ('ret_v', 512), ('ret_g', 512), ('gate_mla', 1024),
              ('gate_gla', 1024), ('gate_ret', 1024))
_IN_OFF = {}
_o = 0
for _n, _w in _IN_WIDTHS:
    _IN_OFF[_n] = _o
    _o += _w
N_IN = _o
LAT_W = 5 * LANE
MIX_W = 8 * 512
GATE_W = 3 * D_MODEL


def _cparams(sem):
    return pltpu.CompilerParams(dimension_semantics=sem, vmem_limit_bytes=VMEM_LIMIT)


def _pick(n, pref, mult=SUBLANE):
    if n <= pref:
        return n
    for t in range(pref - pref % mult, 0, -mult):
        if n % t == 0:
            return t
    return n


def _rope_partner(nrot):
    half = nrot // 2
    q = half // 2
    idx = np.zeros(nrot, np.int32)
    sgn = np.zeros(nrot, np.float32)
    for base in (0, half):
        for i in range(q):
            idx[base + i] = base + i + q
            sgn[base + i] = -1.0
            idx[base + q + i] = base + i
            sgn[base + q + i] = 1.0
    return idx, sgn


def _ada_kernel(c_ref, w_ref, b_ref, o_ref):
    c = c_ref[...]
    cs = c * jax.nn.sigmoid(c)
    o_ref[0] = jnp.dot(cs, w_ref[0], preferred_element_type=F32,
                       precision=lax.Precision.HIGHEST) + b_ref[0]


def _ada_mod(cc, w_ada, b_ada):
    nl, d, n = w_ada.shape
    tn = _pick(n, 1536, LANE)
    return pl.pallas_call(
        _ada_kernel,
        out_shape=jax.ShapeDtypeStruct((nl, cc.shape[0], n), F32),
        grid=(nl, n // tn),
        in_specs=[pl.BlockSpec((cc.shape[0], d), lambda l, j: (0, 0)),
                  pl.BlockSpec((1, d, tn), lambda l, j: (l, 0, j)),
                  pl.BlockSpec((1, 1, tn), lambda l, j: (l, 0, j))],
        out_specs=pl.BlockSpec((1, cc.shape[0], tn), lambda l, j: (l, 0, j)),
        compiler_params=_cparams(("arbitrary", "arbitrary")),
        name="ada_mod",
    )(cc, w_ada, b_ada.reshape(nl, 1, n))


def _nmm_kernel(x_ref, sh_ref, sc_ref, nw_ref, w_ref, o_ref, a_scr):
    @pl.when(pl.program_id(1) == 0)
    def _():
        x = x_ref[...]
        y = x * lax.rsqrt(jnp.mean(x * x, axis=-1, keepdims=True) + EPS) * nw_ref[...]
        a_scr[...] = (y * (1.0 + sc_ref[0]) + sh_ref[0]).astype(BF16)

    o_ref[...] = jnp.dot(a_scr[...], w_ref[...], preferred_element_type=F32).astype(o_ref.dtype)


def _norm_mod_matmul(x, shift, scale, nw, w, out_dtype, tm, tn, name):
    r, d = x.shape
    g = shift.shape[0]
    rpg = r // g
    n = w.shape[1]
    tm = _pick(rpg, tm)
    tn = _pick(n, tn, LANE)
    return pl.pallas_call(
        _nmm_kernel,
        out_shape=jax.ShapeDtypeStruct((r, n), out_dtype),
        grid=(r // tm, n // tn),
        in_specs=[pl.BlockSpec((tm, d), lambda i, j: (i, 0)),
                  pl.BlockSpec((1, 1, d), lambda i, j: (i * tm // rpg, 0, 0)),
                  pl.BlockSpec((1, 1, d), lambda i, j: (i * tm // rpg, 0, 0)),
                  pl.BlockSpec((1, d), lambda i, j: (0, 0)),
                  pl.BlockSpec((d, tn), lambda i, j: (0, j))],
        out_specs=pl.BlockSpec((tm, tn), lambda i, j: (i, j)),
        scratch_shapes=[pltpu.VMEM((tm, d), BF16)],
        compiler_params=_cparams(("parallel", "arbitrary")),
        name=name,
    )(x, shift, scale, nw, w)


def _rms(x, n):
    return lax.rsqrt(jnp.sum(x * x, axis=-1, keepdims=True) * (1.0 / n) + EPS)


def _mla_prep_kernel(cq_ref, ckv_ref, kr_ref, krp_ref, cos_ref, sin_ref, qna_ref, kvna_ref,
                     wq_ref, wk_ref, wv_ref, qn_ref, qnp_ref, kn_ref, knp_ref,
                     q_out, k_out, v_out):
    cos = cos_ref[...]
    sin = sin_ref[...]
    cq = cq_ref[...]
    cqn = (cq * _rms(cq, MLA_Q_LORA) * qna_ref[...]).astype(BF16)
    qx = jnp.dot(cqn, wq_ref[...], preferred_element_type=F32)
    ckv = ckv_ref[...]
    ckvn = (ckv * _rms(ckv, MLA_KV_LORA) * kvna_ref[...]).astype(BF16)
    kx = jnp.dot(ckvn, wk_ref[...], preferred_element_type=F32)
    vx = jnp.dot(ckvn, wv_ref[...], preferred_element_type=F32)
    lane = lax.broadcasted_iota(jnp.int32, kr_ref.shape, 1)
    krm = jnp.where(lane < MLA_QK, kr_ref[...], 0.0)
    krp = krp_ref[...]
    ssq_kr = jnp.sum(krm * krm, axis=-1, keepdims=True)
    qc = qn_ref[...] * cos
    qs = qnp_ref[...] * sin
    kc = kn_ref[...] * cos
    ks = knp_ref[...] * sin
    hw = MLA_HEADS * LANE
    for h in range(MLA_HEADS):
        sl = slice(h * LANE, (h + 1) * LANE)
        xa = qx[:, sl]
        xb = qx[:, hw + h * LANE: hw + (h + 1) * LANE]
        r = _rms(xa, MLA_QK) * (MLA_QK ** -0.5 * LOG2E)
        q_out[0, h] = ((xa * qc + xb * qs) * r).astype(BF16)
        kn = kx[:, sl]
        rk = lax.rsqrt((jnp.sum(kn * kn, axis=-1, keepdims=True) + ssq_kr) * (1.0 / MLA_QK) + EPS)
        k_out[0, h] = (((kn + krm) * kc + krp * ks) * rk).astype(BF16)
        ones_lane = MLA_V if h % 2 == 0 else 0
        v_out[0, h] = jnp.where(lane == ones_lane, 1.0, vx[:, sl]).astype(BF16)


def _mla_prep(lat, cos_t, sin_t, qna, kvna, wq2, wk, wv, qn, qnp, kn, knp, bsz, seq):
    tm = _pick(seq, 512)
    nb = seq // tm
    hw = MLA_HEADS * LANE
    row = lambda b, s: b * nb + s
    const = lambda b, s: (0, 0)
    oshape = jax.ShapeDtypeStruct((bsz, MLA_HEADS, seq, LANE), BF16)
    ospec = pl.BlockSpec((1, MLA_HEADS, tm, LANE), lambda b, s: (b, 0, s, 0))
    return pl.pallas_call(
        _mla_prep_kernel,
        out_shape=(oshape, oshape, oshape),
        grid=(bsz, nb),
        in_specs=[pl.BlockSpec((tm, 2 * LANE), lambda b, s: (row(b, s), 0)),
                  pl.BlockSpec((tm, LANE), lambda b, s: (row(b, s), 2)),
                  pl.BlockSpec((tm, LANE), lambda b, s: (row(b, s), 3)),
                  pl.BlockSpec((tm, LANE), lambda b, s: (row(b, s), 4)),
                  pl.BlockSpec((tm, LANE), lambda b, s: (s, 0)),
                  pl.BlockSpec((tm, LANE), lambda b, s: (s, 0)),
                  pl.BlockSpec((1, MLA_Q_LORA), const),
                  pl.BlockSpec((1, MLA_KV_LORA), const),
                  pl.BlockSpec((MLA_Q_LORA, 2 * hw), const),
                  pl.BlockSpec((MLA_KV_LORA, hw), const),
                  pl.BlockSpec((MLA_KV_LORA, hw), const),
                  pl.BlockSpec((1, LANE), const), pl.BlockSpec((1, LANE), const),
                  pl.BlockSpec((1, LANE), const), pl.BlockSpec((1, LANE), const)],
        out_specs=(ospec, ospec, ospec),
        compiler_params=_cparams(("parallel", "arbitrary")),
        name="mla_prep",
    )(lat, lat, lat, lat, cos_t, sin_t, qna, kvna, wq2, wk, wv, qn, qnp, kn, knp)


def _attn_kernel(q_ref, k_ref, v_ref, o_ref, s_scr, *, tk, nk):
    tq = q_ref.shape[2]
    accs = []
    for hh in range(2):
        q = q_ref[0, hh]

        def scores(c, hh=hh, q=q):
            s = _dot_nt(q, k_ref[0, hh, c * tk:(c + 1) * tk, :])
            s_scr[hh, c % 2] = s
            return jnp.max(s, axis=-1, keepdims=True)

        m = jnp.full((tq, 1), -jnp.inf, F32)
        acc = jnp.zeros((tq, LANE), F32)
        mt = scores(0)
        for c in range(nk):
            mt_next = scores(c + 1) if c + 1 < nk else None
            m_new = jnp.maximum(m, mt)
            p = jnp.exp2(s_scr[hh, c % 2] - m_new)
            acc = jnp.exp2(m - m_new) * acc + jnp.dot(
                p.astype(BF16), v_ref[0, hh, c * tk:(c + 1) * tk, :], preferred_element_type=F32)
            m, mt = m_new, mt_next
        accs.append(acc)
    lane = lax.broadcasted_iota(jnp.int32, (tq, LANE), 1)
    inv0 = 1.0 / accs[0][:, MLA_V:MLA_V + 1]
    inv1 = 1.0 / accs[1][:, 0:1]
    o_ref[0] = jnp.where(lane < MLA_V, accs[0] * inv0, accs[1] * inv1).astype(o_ref.dtype)


def _attention(q, k, v):
    bsz, nh, sq, _ = q.shape
    sk = k.shape[2]
    tq = _pick(sq, 512)
    tk = next(t for t in (768, 512, 256, 128, sk) if sk % t == 0)
    kern = functools.partial(_attn_kernel, tk=tk, nk=sk // tk)
    return pl.pallas_call(
        kern,
        out_shape=jax.ShapeDtypeStruct((bsz, sq, nh * MLA_V), BF16),
        grid=(bsz, nh // 2, sq // tq),
        in_specs=[pl.BlockSpec((1, 2, tq, LANE), lambda b, h, i: (b, h, i, 0)),
                  pl.BlockSpec((1, 2, sk, LANE), lambda b, h, i: (b, h, 0, 0)),
                  pl.BlockSpec((1, 2, sk, LANE), lambda b, h, i: (b, h, 0, 0))],
        out_specs=pl.BlockSpec((1, tq, LANE), lambda b, h, i: (b, i, h)),
        scratch_shapes=[pltpu.VMEM((2, 2, tq, tk), F32)],
        compiler_params=_cparams(("parallel", "parallel", "arbitrary")),
        name="mla_attention",
    )(q, k, v)


def _dot_nt(a, b):
    return lax.dot_general(a, b, (((1,), (1,)), ((), ())), preferred_element_type=F32)


def _dot_tn(a, b):
    return lax.dot_general(a, b, (((0,), (0,)), ((), ())), preferred_element_type=F32)


def _split_bf16(x, n):
    parts = []
    for _ in range(n):
        p = x.astype(BF16)
        parts.append(p)
        x = x - p.astype(F32)
    return parts


def _dot_exact_lhs(a, x):
    hi, mid, lo = _split_bf16(x, 3)
    d = lambda p: jnp.dot(a, p, preferred_element_type=F32)
    return (d(lo) + d(mid)) + d(hi)


def _dot_x3(a, b):
    a_hi, a_lo = _split_bf16(a, 2)
    b_hi, b_lo = _split_bf16(b, 2)
    d = lambda p, r: jnp.dot(p, r, preferred_element_type=F32)
    return (d(a_lo, b_hi) + d(a_hi, b_lo)) + d(a_hi, b_hi)


def _scan_kernel(*refs, kind, reverse, nchunk, epilogue):
    it = iter(refs)
    q_ref, k_ref, v_ref = next(it), next(it), next(it)
    if kind == 'gla':
        r_ref, w2_ref, bg_ref = next(it), next(it), next(it)
    else:
        cos_ref, sin_ref, rd_ref = next(it), next(it), next(it)
    s0_ref = next(it)
    if epilogue:
        oo_ref, g_ref, nw_ref = next(it), next(it), next(it)
    o_ref, sf_ref, st_scr = next(it), next(it), next(it)

    step = pl.program_id(1)

    @pl.when(step == 0)
    def _():
        st_scr[...] = s0_ref[0]

    blk = nchunk * CHUNK
    ii = lax.broadcasted_iota(jnp.int32, (blk, blk), 0)
    jj = lax.broadcasted_iota(jnp.int32, (blk, blk), 1)
    same = (ii // CHUNK) == (jj // CHUNK)
    if reverse:
        amask = same & (jj > ii)
    else:
        amask = same & (ii >= jj)
    order = range(nchunk - 1, -1, -1) if reverse else range(nchunk)
    vb = v_ref[...].astype(BF16)

    if kind == 'gla':
        pre = _dot_x3(r_ref[...], w2_ref[...]) + bg_ref[...]
        la = jax.nn.log_sigmoid(pre) * (1.0 / GLA_GATE_NORMALIZER)
        tri = (same & ((jj >= ii) if reverse else (ii >= jj))).astype(BF16)
        cum = _dot_exact_lhs(tri, la)
        totb = _dot_exact_lhs(same.astype(BF16), la)
        qd = (q_ref[...] * (GLA_DK ** -0.5) * jnp.exp(cum)).astype(BF16)
        kd = (k_ref[...] * jnp.exp(-cum)).astype(BF16)
        kz = (k_ref[...] * jnp.exp(totb - cum)).astype(BF16)
        dtot = jnp.exp(totb)
    else:
        pidx = (lax.broadcasted_iota(jnp.int32, (blk, LANE), 0) % CHUNK).astype(F32)
        dd = jnp.where(amask, (jj - ii if reverse else ii - jj).astype(F32), 0.0)
        cs = cos_ref[...]
        sn = sin_ref[...]

    for h in range(4):
        cols = slice(h * LANE, (h + 1) * LANE)
        vh = vb[:, cols]
        if kind == 'gla':
            qh, kzh = qd[:, cols], kz[:, cols]
            att = jnp.where(amask, _dot_nt(qh, kd[:, cols]), 0.0)
            xi = None
        else:
            lg = -jnp.exp(rd_ref[h:h + 1, :])
            q = q_ref[:, cols]
            k = k_ref[:, cols]
            qh = (q * cs + pltpu.roll(q, LANE // 2, 1) * sn).astype(BF16)
            kr = (k * cs + pltpu.roll(k, LANE // 2, 1) * sn) * (RET_DK ** -0.5)
            if reverse:
                zeta = jnp.exp(pidx * lg)
                xi = jnp.exp((CHUNK - pidx) * lg)
            else:
                zeta = jnp.exp((CHUNK - 1 - pidx) * lg)
                xi = jnp.exp((pidx + 1.0) * lg)
            kzh = (kr * zeta).astype(BF16)
            lgb = jnp.concatenate([lg] * (blk // LANE), axis=1) if blk > LANE else lg[:, :blk]
            att = _dot_nt(qh, kr.astype(BF16)) * jnp.where(amask, jnp.exp(dd * lgb), 0.0)
            gch = jnp.exp(CHUNK * lg)
        o_intra = jnp.dot(att.astype(BF16), vh, preferred_element_type=F32)
        st = st_scr[h]
        for c in order:
            rows = slice(c * CHUNK, (c + 1) * CHUNK)
            o = o_intra[rows] + (_dot_nt(qh[rows], st.astype(BF16)) if xi is None
                                 else _dot_nt(qh[rows], st.astype(BF16)) * xi[rows])
            decay = dtot[c * CHUNK:c * CHUNK + 1, cols] if kind == 'gla' else gch
            st = st * decay + _dot_tn(vh[rows], kzh[rows])
            if epilogue:
                osum = o + oo_ref[rows, cols]
                g = g_ref[rows, cols]
                y = osum * _rms(osum, LANE) * nw_ref[...]
                o_ref[rows, cols] = (y * (g * jax.nn.sigmoid(g))).astype(o_ref.dtype)
            else:
                o_ref[rows, cols] = o
        st_scr[h] = st

    @pl.when(step == pl.num_programs(1) - 1)
    def _():
        sf_ref[0] = st_scr[...]


def _scan_pass(kind, reverse, mix, col0, bsz, seq, s0, extra, epi):
    blk = _pick(seq, 256, CHUNK)
    nb = seq // blk
    pos = (lambda s: nb - 1 - s) if reverse else (lambda s: s)
    row = lambda b, s: b * nb + pos(s)
    const = lambda b, s: (0, 0)
    in_specs = [pl.BlockSpec((blk, 512), lambda b, s, c=col0 + j: (row(b, s), c)) for j in range(3)]
    args = [mix, mix, mix]
    if kind == 'gla':
        lat, w2p, bg = extra
        in_specs += [pl.BlockSpec((blk, LANE), lambda b, s: (row(b, s), 3)),
                     pl.BlockSpec((LANE, 512), const), pl.BlockSpec((1, 512), const)]
        args += [lat, w2p, bg]
    else:
        cos2, sin2, rd = extra
        in_specs += [pl.BlockSpec((blk, LANE), lambda b, s: (pos(s), 0)),
                     pl.BlockSpec((blk, LANE), lambda b, s: (pos(s), 0)),
                     pl.BlockSpec((SUBLANE, LANE), const)]
        args += [cos2, sin2, rd]
    in_specs.append(pl.BlockSpec((1, 4, LANE, LANE), lambda b, s: (b, 0, 0, 0)))
    args.append(s0)
    if epi is not None:
        oo, gcol, nw = epi
        in_specs += [pl.BlockSpec((blk, 512), lambda b, s: (row(b, s), 0)),
                     pl.BlockSpec((blk, 512), lambda b, s: (row(b, s), gcol)),
                     pl.BlockSpec((1, LANE), const)]
        args += [oo, mix, nw]
    r = bsz * seq
    kern = functools.partial(_scan_kernel, kind=kind, reverse=reverse, nchunk=blk // CHUNK,
                             epilogue=epi is not None)
    return pl.pallas_call(
        kern,
        out_shape=(jax.ShapeDtypeStruct((r, 512), BF16 if epi is not None else F32),
                   jax.ShapeDtypeStruct((bsz, 4, LANE, LANE), F32)),
        grid=(bsz, nb),
        in_specs=in_specs,
        out_specs=(pl.BlockSpec((blk, 512), lambda b, s: (row(b, s), 0)),
                   pl.BlockSpec((1, 4, LANE, LANE), lambda b, s: (b, 0, 0, 0))),
        scratch_shapes=[pltpu.VMEM((4, LANE, LANE), F32)],
        compiler_params=_cparams(("parallel", "arbitrary")),
        name=f"{kind}_scan_{'bwd' if reverse else 'fwd'}",
    )(*args)


def _merge_kernel(y0_ref, y1_ref, y2_ref, g_ref, bg_ref, wb_ref, wo_ref, h_ref, gt_ref, o_ref):
    u = None
    for n, y_ref in enumerate((y0_ref, y1_ref, y2_ref)):
        cols = slice(n * D_MODEL, (n + 1) * D_MODEL)
        t = jnp.dot(y_ref[...], wb_ref[n], preferred_element_type=F32)
        t = jax.nn.sigmoid(g_ref[:, cols] + bg_ref[:, cols]) * t
        u = t if u is None else u + t
    out = jnp.dot(u.astype(BF16), wo_ref[...], preferred_element_type=F32)
    o_ref[...] = h_ref[...] + gt_ref[0] * out


def _merge(ys, gates, bg, wb, wo, h, gt):
    r, d = h.shape
    g = gt.shape[0]
    rpg = r // g
    tm = _pick(rpg, 512)
    yspec = pl.BlockSpec((tm, BRANCH_W), lambda i: (i, 0))
    return pl.pallas_call(
        _merge_kernel,
        out_shape=jax.ShapeDtypeStruct((r, d), F32),
        grid=(r // tm,),
        in_specs=[yspec, yspec, yspec,
                  pl.BlockSpec((tm, GATE_W), lambda i: (i, 0)),
                  pl.BlockSpec((1, GATE_W), lambda i: (0, 0)),
                  pl.BlockSpec((N_BRANCH, BRANCH_W, d), lambda i: (0, 0, 0)),
                  pl.BlockSpec((d, d), lambda i: (0, 0)),
                  pl.BlockSpec((tm, d), lambda i: (i, 0)),
                  pl.BlockSpec((1, 1, d), lambda i: (i * tm // rpg, 0, 0))],
        out_specs=pl.BlockSpec((tm, d), lambda i: (i, 0)),
        compiler_params=_cparams(("parallel",)),
        name="gated_merge",
    )(ys[0], ys[1], ys[2], gates, bg, wb, wo, h, gt)


def _ffn_kernel(hp_ref, h_ref, hn_ref, sh_ref, sc_ref, gt_ref, nw_ref, wg_ref, wu_ref, wdw_ref,
                bdw_ref, wo_ref, o_ref, a_scr, acc_scr, *, tm, seq):
    i = pl.program_id(0)
    f = pl.program_id(1)

    def norm_mod(x):
        y = x * lax.rsqrt(jnp.mean(x * x, axis=-1, keepdims=True) + EPS) * nw_ref[...]
        return (y * (1.0 + sc_ref[0]) + sh_ref[0]).astype(BF16)

    @pl.when(f == 0)
    def _():
        a_scr[0:HALO, :] = norm_mod(hp_ref[...])
        a_scr[HALO:HALO + tm, :] = norm_mod(h_ref[...])
        a_scr[HALO + tm:, :] = norm_mod(hn_ref[...])
        acc_scr[...] = jnp.zeros_like(acc_scr)

    gate = jnp.dot(a_scr[...], wg_ref[...], preferred_element_type=F32)
    up = jnp.dot(a_scr[HALO:HALO + tm, :], wu_ref[...], preferred_element_type=F32)
    row0 = i * tm
    prev_ok = (row0 % seq != 0).astype(F32)
    next_ok = ((row0 + tm) % seq != 0).astype(F32)
    rid = lax.broadcasted_iota(jnp.int32, (tm + 2 * HALO, 1), 0)
    rscale = jnp.where(rid == HALO - 1, prev_ok, jnp.where(rid == HALO + tm, next_ok, 1.0))
    gate = gate * rscale
    g_prev = pltpu.roll(gate, 1, 0)[HALO:HALO + tm]
    g_next = pltpu.roll(gate, tm + 2 * HALO - 1, 0)[HALO:HALO + tm]
    wdw = wdw_ref[...]
    conv = (g_prev * wdw[0:1] + gate[HALO:HALO + tm] * wdw[1:2] + g_next * wdw[2:3]
            + bdw_ref[...])
    act = jax.nn.gelu(conv, approximate=True) * up
    acc_scr[...] += jnp.dot(act.astype(BF16), wo_ref[...], preferred_element_type=F32)

    @pl.when(f == pl.num_programs(1) - 1)
    def _():
        o_ref[...] = h_ref[...] + gt_ref[0] * acc_scr[...]


def _ffn(h, shift, scale, gt, nw, wg, wu, wdw, bdw, wo, seq):
    r, d = h.shape
    g = shift.shape[0]
    rpg = r // g
    dff = wg.shape[1]
    tm = _pick(seq, 512, HALO)
    tf = _pick(dff, 1408, LANE)
    nsub = tm // HALO
    nrb = r // HALO
    gidx = lambda i, f: (i * tm // rpg, 0, 0)
    kern = functools.partial(_ffn_kernel, tm=tm, seq=seq)
    return pl.pallas_call(
        kern,
        out_shape=jax.ShapeDtypeStruct((r, d), F32),
        grid=(r // tm, dff // tf),
        in_specs=[pl.BlockSpec((HALO, d), lambda i, f: (jnp.maximum(i * nsub - 1, 0), 0)),
                  pl.BlockSpec((tm, d), lambda i, f: (i, 0)),
                  pl.BlockSpec((HALO, d), lambda i, f: (jnp.minimum((i + 1) * nsub, nrb - 1), 0)),
                  pl.BlockSpec((1, 1, d), gidx), pl.BlockSpec((1, 1, d), gidx),
                  pl.BlockSpec((1, 1, d), gidx),
                  pl.BlockSpec((1, d), lambda i, f: (0, 0)),
                  pl.BlockSpec((d, tf), lambda i, f: (0, f)),
                  pl.BlockSpec((d, tf), lambda i, f: (0, f)),
                  pl.BlockSpec((3, tf), lambda i, f: (0, f)),
                  pl.BlockSpec((1, tf), lambda i, f: (0, f)),
                  pl.BlockSpec((tf, d), lambda i, f: (f, 0))],
        out_specs=pl.BlockSpec((tm, d), lambda i, f: (i, 0)),
        scratch_shapes=[pltpu.VMEM((tm + 2 * HALO, d), BF16), pltpu.VMEM((tm, d), F32)],
        compiler_params=_cparams(("parallel", "arbitrary")),
        name="conv_ffn",
    )(h, h, h, shift, scale, gt, nw, wg, wu, wdw, bdw, wo)


def _in_proj_layout():
    idx, sgn = [], []

    def take(name, width, off=0):
        idx.extend(range(_IN_OFF[name] + off, _IN_OFF[name] + off + width))
        sgn.extend([1.0] * width)

    def zeros(width):
        idx.extend([0] * width)
        sgn.extend([0.0] * width)

    take('mla_q', 256)
    take('mla_kv', 128)
    zeros(64); take('mla_kr', 32); take('gla_rf', 16); take('gla_rb', 16)
    zeros(64)
    pidx, psgn = _rope_partner(MLA_ROPE)
    idx.extend((_IN_OFF['mla_kr'] + pidx).tolist()); sgn.extend(psgn.tolist())
    zeros(32)
    for name in ('gla_q', 'gla_k', 'gla_v', 'gla_g', 'ret_q', 'ret_k', 'ret_v', 'ret_g'):
        take(name, 512)
    for name in ('gate_mla', 'gate_gla', 'gate_ret'):
        take(name, 1024)
    return np.asarray(idx, np.int32), np.asarray(sgn, np.float32)


def _mla_weight_layout():
    pidx, psgn = _rope_partner(MLA_ROPE)
    qa_idx, qa_sgn, qb_idx, qb_sgn = [], [], [], []
    k_idx, k_sgn, v_idx, v_sgn = [], [], [], []
    for h in range(MLA_HEADS):
        base = h * MLA_QK
        qa_idx += list(range(base, base + MLA_QK)) + [0] * 32
        qa_sgn += [1.0] * MLA_QK + [0.0] * 32
        qb_idx += [0] * 64 + (base + MLA_NOPE + pidx).tolist() + [0] * 32
        qb_sgn += [0.0] * 64 + psgn.tolist() + [0.0] * 32
        kb = h * (MLA_NOPE + MLA_V)
        k_idx += list(range(kb, kb + MLA_NOPE)) + [0] * 64
        k_sgn += [1.0] * 64 + [0.0] * 64
        vcols = list(range(kb + MLA_NOPE, kb + MLA_NOPE + MLA_V))
        if h % 2 == 0:
            v_idx += vcols + [0] * 64
            v_sgn += [1.0] * 64 + [0.0] * 64
        else:
            v_idx += [0] * 64 + vcols
            v_sgn += [0.0] * 64 + [1.0] * 64
    f = lambda a, t: np.asarray(a, t)
    return (f(qa_idx + qb_idx, np.int32), f(qa_sgn + qb_sgn, np.float32),
            f(k_idx, np.int32), f(k_sgn, np.float32), f(v_idx, np.int32), f(v_sgn, np.float32))


def _slot_vec(w):
    pidx, _ = _rope_partner(MLA_ROPE)
    z32 = jnp.zeros((32,), F32)
    a = jnp.concatenate([w, z32])
    p = jnp.concatenate([jnp.zeros((64,), F32), w[MLA_NOPE + pidx], z32])
    return a.reshape(1, LANE), p.reshape(1, LANE)


def _mla_tables(seq, rope):
    ones = jnp.ones((seq, 64), F32)
    if not rope:
        return jnp.ones((seq, LANE), F32), jnp.zeros((seq, LANE), F32)
    pos = jnp.arange(seq)
    dim = MLA_ROPE // 2
    inv = ROPE_THETA ** (-jnp.arange(dim // 2, dtype=F32) * 2.0 / dim)
    ar = (pos // GRID_W).astype(F32)[:, None] * inv[None, :]
    ac = (pos % GRID_W).astype(F32)[:, None] * inv[None, :]
    cos = jnp.concatenate([ones, jnp.cos(ar), jnp.cos(ar), jnp.cos(ac), jnp.cos(ac),
                           jnp.ones((seq, 32), F32)], axis=1)
    sin = jnp.concatenate([0.0 * ones, jnp.sin(ar), jnp.sin(ar), jnp.sin(ac), jnp.sin(ac),
                           jnp.zeros((seq, 32), F32)], axis=1)
    return cos, sin


def _ret_tables(start, seq):
    inv = 1.0 / (RET_THETA ** jnp.linspace(0.0, 1.0, RET_DK // 2, dtype=F32))
    ang = (start + jnp.arange(seq)).astype(F32)[:, None] * inv[None, :]
    cos, sin = jnp.cos(ang), jnp.sin(ang)
    return jnp.concatenate([cos, cos], axis=1), jnp.concatenate([-sin, sin], axis=1)


def _mixers(lat, mix, bsz, seq, lw, mla_tabs, ret_tabs, states):
    q, k, v = _mla_prep(lat, mla_tabs[0], mla_tabs[1], lw['qna'], lw['kvna'], lw['wq2'], lw['wk'],
                        lw['wv'], lw['qn'], lw['qnp'], lw['kn'], lw['knp'], bsz, seq)
    sg_f, sg_b, sr_f, sr_b = states
    ob, sg_b2 = _scan_pass('gla', True, mix, 0, bsz, seq, sg_b, (lat, lw['w2b'], lw['bgb']), None)
    y_gla, sg_f2 = _scan_pass('gla', False, mix, 0, bsz, seq, sg_f, (lat, lw['w2f'], lw['bgf']),
                              (ob, 3, lw['gla_on']))
    rb, sr_b2 = _scan_pass('ret', True, mix, 4, bsz, seq, sr_b,
                           (ret_tabs[0], ret_tabs[1], lw['rd_b']), None)
    y_ret, sr_f2 = _scan_pass('ret', False, mix, 4, bsz, seq, sr_f,
                              (ret_tabs[0], ret_tabs[1], lw['rd_f']), (rb, 7, lw['ones']))
    return (q, k, v), y_gla, y_ret, (sg_f2, sg_b2, sr_f2, sr_b2)


def kernel(x, c, ctx, c_ctx, w_ada, b_ada, norm1_w, norm2_w, w_in, b_gate, mla_q_norm_a, mla_w_qb,
           mla_kv_norm_a, mla_w_kvb, mla_q_norm, mla_k_norm, gla_w_gk2, gla_b_gk, gla_o_norm,
           ret_decay, w_branch, w_out, w_ffn_in, w_dw, b_dw, w_ffn_out):
    bsz, seq, d = x.shape
    clen = ctx.shape[1]
    depth = w_ada.shape[0]
    r_lat, r_ctx = bsz * seq, bsz * clen

    npad = -(bsz + 1) % SUBLANE
    cc = jnp.concatenate([c, c_ctx[None, :], jnp.zeros((npad, d), F32)], axis=0)
    mod_all = _ada_mod(cc, w_ada, b_ada)

    in_idx, in_sgn = _in_proj_layout()
    qidx, qsgn, kidx, ksgn, vidx, vsgn = _mla_weight_layout()
    mla_lat_tabs = _mla_tables(seq, True)
    mla_ctx_tabs = _mla_tables(clen, False)
    ret_lat_tabs = _ret_tables(clen, seq)
    ret_ctx_tabs = _ret_tables(0, clen)
    zstate = jnp.zeros((bsz, 4, LANE, LANE), F32)
    ones128 = jnp.ones((1, LANE), F32)

    h = x.reshape(r_lat, d)
    hc = ctx.reshape(r_ctx, d)
    for l in range(depth):
        need_ctx = l < depth - 1
        mods = mod_all[l].reshape(-1, 6, d)
        ml = [mods[:bsz, j].reshape(bsz, 1, d) for j in range(6)]
        mc = [mods[bsz:bsz + 1, j].reshape(1, 1, d) for j in range(6)]

        w_p = (w_in[l][:, in_idx] * in_sgn[None, :]).astype(BF16)
        w_lat, w_mix, w_gate = w_p[:, :LAT_W], w_p[:, LAT_W:LAT_W + MIX_W], w_p[:, LAT_W + MIX_W:]
        qn, qnp = _slot_vec(mla_q_norm[l])
        kn, knp = _slot_vec(mla_k_norm[l])
        w2 = gla_w_gk2[l]
        zr = jnp.zeros((LANE - 2 * GLA_GATE_RANK, 4 * GLA_DK), F32)
        z16 = jnp.zeros((GLA_GATE_RANK, 4 * GLA_DK), F32)
        lw = dict(
            qna=mla_q_norm_a[l].reshape(1, -1), kvna=mla_kv_norm_a[l].reshape(1, -1),
            wq2=(mla_w_qb[l][:, qidx] * qsgn[None, :]).astype(BF16),
            wk=(mla_w_kvb[l][:, kidx] * ksgn[None, :]).astype(BF16),
            wv=(mla_w_kvb[l][:, vidx] * vsgn[None, :]).astype(BF16),
            qn=qn, qnp=qnp, kn=kn, knp=knp,
            w2f=jnp.concatenate([zr, w2[0], z16], axis=0),
            w2b=jnp.concatenate([zr, z16, w2[1]], axis=0),
            bgf=gla_b_gk[l, 0].reshape(1, -1), bgb=gla_b_gk[l, 1].reshape(1, -1),
            gla_on=gla_o_norm[l].reshape(1, LANE), ones=ones128,
            rd_f=jnp.broadcast_to(jnp.pad(ret_decay[l, 0], (0, 4))[:, None], (SUBLANE, LANE)),
            rd_b=jnp.broadcast_to(jnp.pad(ret_decay[l, 1], (0, 4))[:, None], (SUBLANE, LANE)),
        )
        nw1 = norm1_w[l].reshape(1, d)
        nw2 = norm2_w[l].reshape(1, d)
        bg = b_gate[l].reshape(1, GATE_W)
        wb = w_branch[l].astype(BF16)
        wo = w_out[l].astype(BF16)
        wg = w_ffn_in[l][:, :D_FF].astype(BF16)
        wu = w_ffn_in[l][:, D_FF:].astype(BF16)
        wfo = w_ffn_out[l].astype(BF16)
        bdw = b_dw[l].reshape(1, D_FF)

        def project(hh, m, name):
            lat = _norm_mod_matmul(hh, m[0], m[1], nw1, w_lat, F32, 1024, LAT_W, name + "_lat")
            mix = _norm_mod_matmul(hh, m[0], m[1], nw1, w_mix, F32, 1024, 1024, name + "_mix")
            return lat, mix

        lat_c, mix_c = project(hc, mc, "in_proj_ctx")
        (q_c, k_c, v_c), yg_c, yr_c, states = _mixers(
            lat_c, mix_c, bsz, clen, lw, mla_ctx_tabs, ret_ctx_tabs, (zstate,) * 4)
        lat_l, mix_l = project(h, ml, "in_proj")
        (q_l, k_l, v_l), yg_l, yr_l, _ = _mixers(
            lat_l, mix_l, bsz, seq, lw, mla_lat_tabs, ret_lat_tabs, states)
        y_mla = _attention(q_l, jnp.concatenate([k_c, k_l], axis=2),
                           jnp.concatenate([v_c, v_l], axis=2)).reshape(r_lat, BRANCH_W)
        gates_l = _norm_mod_matmul(h, ml[0], ml[1], nw1, w_gate, F32, 1024, 1024, "in_proj_gate")
        h = _merge((y_mla, yg_l, yr_l), gates_l, bg, wb, wo, h, ml[2])
        h = _ffn(h, ml[3], ml[4], ml[5], nw2, wg, wu, w_dw[l], bdw, wfo, seq)
        if need_ctx:
            ym_c = _attention(q_c, k_c, v_c).reshape(r_ctx, BRANCH_W)
            gates_c = _norm_mod_matmul(hc, mc[0], mc[1], nw1, w_gate, F32, 1024, 1024,
                                       "in_proj_gate_ctx")
            hc = _merge((ym_c, yg_c, yr_c), gates_c, bg, wb, wo, hc, mc[2])
            hc = _ffn(hc, mc[3], mc[4], mc[5], nw2, wg, wu, w_dw[l], bdw, wfo, clen)
    return h.reshape(bsz, seq, d)
```

```python
import functools
import math

import numpy as np
import jax
import jax.numpy as jnp
from jax import lax
from jax.experimental import pallas as pl
from jax.experimental.pallas import tpu as pltpu

F32 = jnp.float32
BF16 = jnp.bfloat16

D_MODEL = 1024
GRID_W = 64
N_BRANCH = 3
BRANCH_W = 512
MLA_HEADS = 8
MLA_NOPE = 64
MLA_ROPE = 32
MLA_QK = MLA_NOPE + MLA_ROPE
MLA_V = BRANCH_W // MLA_HEADS
MLA_Q_LORA = 256
MLA_KV_LORA = 128
GLA_HEADS = 4
GLA_DK = 128
GLA_DV = 128
GLA_GATE_RANK = 16
GLA_GATE_NORMALIZER = 16.0
RET_HEADS = 4
RET_DK = 128
RET_DV = 128
D_FF = 2816
CHUNK = 64
ROPE_THETA = 10000.0
RET_THETA = 10000.0
EPS = 1e-6
LOG2E = math.log2(math.e)
LANE = 128
SUBLANE = 8
HALO = 16
VMEM_LIMIT = 56 * 1024 * 1024

_IN_WIDTHS = (('mla_q', 256), ('mla_kv', 128), ('mla_kr', 32), ('gla_q', 512), ('gla_k', 512),
              ('gla_v', 512), ('gla_g', 512), ('gla_rf', 16), ('gla_rb', 16), ('ret_q', 512),
              ('ret_k', 512), ('ret_v', 512), ('ret_g', 512), ('gate_mla', 1024),
              ('gate_gla', 1024), ('gate_ret', 1024))
_IN_OFF = {}
_o = 0
for _n, _w in _IN_WIDTHS:
    _IN_OFF[_n] = _o
    _o += _w
N_IN = _o
LAT_W = 5 * LANE
MIX_W = 8 * 512
GATE_W = 3 * D_MODEL


def _cparams(sem):
    return pltpu.CompilerParams(dimension_semantics=sem, vmem_limit_bytes=VMEM_LIMIT)


def _pick(n, pref, mult=SUBLANE):
    if n <= pref:
        return n
    for t in range(pref - pref % mult, 0, -mult):
        if n % t == 0:
            return t
    return n


def _rope_partner(nrot):
    half = nrot // 2
    q = half // 2
    idx = np.zeros(nrot, np.int32)
    sgn = np.zeros(nrot, np.float32)
    for base in (0, half):
        for i in range(q):
            idx[base + i] = base + i + q
            sgn[base + i] = -1.0
            idx[base + q + i] = base + i
            sgn[base + q + i] = 1.0
    return idx, sgn


def _ada_kernel(c_ref, w_ref, b_ref, o_ref):
    c = c_ref[...]
    cs = c * jax.nn.sigmoid(c)
    o_ref[0] = jnp.dot(cs, w_ref[0], preferred_element_type=F32,
                       precision=lax.Precision.HIGHEST) + b_ref[0]


def _ada_mod(cc, w_ada, b_ada):
    nl, d, n = w_ada.shape
    tn = _pick(n, 1536, LANE)
    return pl.pallas_call(
        _ada_kernel,
        out_shape=jax.ShapeDtypeStruct((nl, cc.shape[0], n), F32),
        grid=(nl, n // tn),
        in_specs=[pl.BlockSpec((cc.shape[0], d), lambda l, j: (0, 0)),
                  pl.BlockSpec((1, d, tn), lambda l, j: (l, 0, j)),
                  pl.BlockSpec((1, 1, tn), lambda l, j: (l, 0, j))],
        out_specs=pl.BlockSpec((1, cc.shape[0], tn), lambda l, j: (l, 0, j)),
        compiler_params=_cparams(("arbitrary", "arbitrary")),
        name="ada_mod",
    )(cc, w_ada, b_ada.reshape(nl, 1, n))


def _nmm_kernel(x_ref, sh_ref, sc_ref, nw_ref, w_ref, o_ref, a_scr):
    @pl.when(pl.program_id(1) == 0)
    def _():
        x = x_ref[...]
        y = x * lax.rsqrt(jnp.mean(x * x, axis=-1, keepdims=True) + EPS) * nw_ref[...]
        a_scr[...] = (y * (1.0 + sc_ref[0]) + sh_ref[0]).astype(BF16)

    o_ref[...] = jnp.dot(a_scr[...], w_ref[...], preferred_element_type=F32).astype(o_ref.dtype)


def _norm_mod_matmul(x, shift, scale, nw, w, out_dtype, tm, tn, name):
    r, d = x.shape
    g = shift.shape[0]
    rpg = r // g
    n = w.shape[1]
    tm = _pick(rpg, tm)
    tn = _pick(n, tn, LANE)
    return pl.pallas_call(
        _nmm_kernel,
        out_shape=jax.ShapeDtypeStruct((r, n), out_dtype),
        grid=(r // tm, n // tn),
        in_specs=[pl.BlockSpec((tm, d), lambda i, j: (i, 0)),
                  pl.BlockSpec((1, 1, d), lambda i, j: (i * tm // rpg, 0, 0)),
                  pl.BlockSpec((1, 1, d), lambda i, j: (i * tm // rpg, 0, 0)),
                  pl.BlockSpec((1, d), lambda i, j: (0, 0)),
                  pl.BlockSpec((d, tn), lambda i, j: (0, j))],
        out_specs=pl.BlockSpec((tm, tn), lambda i, j: (i, j)),
        scratch_shapes=[pltpu.VMEM((tm, d), BF16)],
        compiler_params=_cparams(("parallel", "arbitrary")),
        name=name,
    )(x, shift, scale, nw, w)


def _rms(x, n):
    return lax.rsqrt(jnp.sum(x * x, axis=-1, keepdims=True) * (1.0 / n) + EPS)


def _mla_prep_kernel(cq_ref, ckv_ref, kr_ref, krp_ref, cos_ref, sin_ref, qna_ref, kvna_ref,
                     wq_ref, wk_ref, wv_ref, qn_ref, qnp_ref, kn_ref, knp_ref,
                     q_out, k_out, v_out):
    cos = cos_ref[...]
    sin = sin_ref[...]
    cq = cq_ref[...]
    cqn = (cq * _rms(cq, MLA_Q_LORA) * qna_ref[...]).astype(BF16)
    qx = jnp.dot(cqn, wq_ref[...], preferred_element_type=F32)
    ckv = ckv_ref[...]
    ckvn = (ckv * _rms(ckv, MLA_KV_LORA) * kvna_ref[...]).astype(BF16)
    kx = jnp.dot(ckvn, wk_ref[...], preferred_element_type=F32)
    vx = jnp.dot(ckvn, wv_ref[...], preferred_element_type=F32)
    lane = lax.broadcasted_iota(jnp.int32, kr_ref.shape, 1)
    krm = jnp.where(lane < MLA_QK, kr_ref[...], 0.0)
    krp = krp_ref[...]
    ssq_kr = jnp.sum(krm * krm, axis=-1, keepdims=True)
    qc = qn_ref[...] * cos
    qs = qnp_ref[...] * sin
    kc = kn_ref[...] * cos
    ks = knp_ref[...] * sin
    hw = MLA_HEADS * LANE
    for h in range(MLA_HEADS):
        sl = slice(h * LANE, (h + 1) * LANE)
        xa = qx[:, sl]
        xb = qx[:, hw + h * LANE: hw + (h + 1) * LANE]
        r = _rms(xa, MLA_QK) * (MLA_QK ** -0.5 * LOG2E)
        q_out[0, h] = ((xa * qc + xb * qs) * r).astype(BF16)
        kn = kx[:, sl]
        rk = lax.rsqrt((jnp.sum(kn * kn, axis=-1, keepdims=True) + ssq_kr) * (1.0 / MLA_QK) + EPS)
        k_out[0, h] = (((kn + krm) * kc + krp * ks) * rk).astype(BF16)
        ones_lane = MLA_V if h % 2 == 0 else 0
        v_out[0, h] = jnp.where(lane == ones_lane, 1.0, vx[:, sl]).astype(BF16)


def _mla_prep(lat, cos_t, sin_t, qna, kvna, wq2, wk, wv, qn, qnp, kn, knp, bsz, seq):
    tm = _pick(seq, 512)
    nb = seq // tm
    hw = MLA_HEADS * LANE
    row = lambda b, s: b * nb + s
    const = lambda b, s: (0, 0)
    oshape = jax.ShapeDtypeStruct((bsz, MLA_HEADS, seq, LANE), BF16)
    ospec = pl.BlockSpec((1, MLA_HEADS, tm, LANE), lambda b, s: (b, 0, s, 0))
    return pl.pallas_call(
        _mla_prep_kernel,
        out_shape=(oshape, oshape, oshape),
        grid=(bsz, nb),
        in_specs=[pl.BlockSpec((tm, 2 * LANE), lambda b, s: (row(b, s), 0)),
                  pl.BlockSpec((tm, LANE), lambda b, s: (row(b, s), 2)),
                  pl.BlockSpec((tm, LANE), lambda b, s: (row(b, s), 3)),
                  pl.BlockSpec((tm, LANE), lambda b, s: (row(b, s), 4)),
                  pl.BlockSpec((tm, LANE), lambda b, s: (s, 0)),
                  pl.BlockSpec((tm, LANE), lambda b, s: (s, 0)),
                  pl.BlockSpec((1, MLA_Q_LORA), const),
                  pl.BlockSpec((1, MLA_KV_LORA), const),
                  pl.BlockSpec((MLA_Q_LORA, 2 * hw), const),
                  pl.BlockSpec((MLA_KV_LORA, hw), const),
                  pl.BlockSpec((MLA_KV_LORA, hw), const),
                  pl.BlockSpec((1, LANE), const), pl.BlockSpec((1, LANE), const),
                  pl.BlockSpec((1, LANE), const), pl.BlockSpec((1, LANE), const)],
        out_specs=(ospec, ospec, ospec),
        compiler_params=_cparams(("parallel", "arbitrary")),
        name="mla_prep",
    )(lat, lat, lat, lat, cos_t, sin_t, qna, kvna, wq2, wk, wv, qn, qnp, kn, knp)


def _attn_kernel(*refs, chunks):
    q_ref, kv, o_ref, s_scr = refs[0], refs[1:-2], refs[-2], refs[-1]
    tq = q_ref.shape[2]
    accs = []
    for hh in range(2):
        q = q_ref[0, hh]

        def scores(c, hh=hh, q=q):
            src, start, size = chunks[c]
            s = _dot_nt(q, kv[2 * src][0, hh, start:start + size, :])
            s_scr[hh, c % 2, :, :size] = s
            return jnp.max(s, axis=-1, keepdims=True)

        m = jnp.full((tq, 1), -jnp.inf, F32)
        acc = jnp.zeros((tq, LANE), F32)
        mt = scores(0)
        for c, (src, start, size) in enumerate(chunks):
            mt_next = scores(c + 1) if c + 1 < len(chunks) else None
            m_new = jnp.maximum(m, mt)
            p = jnp.exp2(s_scr[hh, c % 2, :, :size] - m_new)
            acc = jnp.exp2(m - m_new) * acc + jnp.dot(
                p.astype(BF16), kv[2 * src + 1][0, hh, start:start + size, :],
                preferred_element_type=F32)
            m, mt = m_new, mt_next
        accs.append(acc)
    lane = lax.broadcasted_iota(jnp.int32, (tq, LANE), 1)
    inv0 = 1.0 / accs[0][:, MLA_V:MLA_V + 1]
    inv1 = 1.0 / accs[1][:, 0:1]
    o_ref[0] = jnp.where(lane < MLA_V, accs[0] * inv0, accs[1] * inv1).astype(o_ref.dtype)


def _attention(q, kvs):
    bsz, nh, sq, _ = q.shape
    tq = _pick(sq, 512)
    chunks, in_specs, args = [], [pl.BlockSpec((1, 2, tq, LANE), lambda b, h, i: (b, h, i, 0))], [q]
    for src, (k, v) in enumerate(kvs):
        sk = k.shape[2]
        tk = next(t for t in (1024, 512, 256, 128, sk) if sk % t == 0)
        chunks += [(src, start, tk) for start in range(0, sk, tk)]
        in_specs += [pl.BlockSpec((1, 2, sk, LANE), lambda b, h, i: (b, h, 0, 0))] * 2
        args += [k, v]
    tk_max = max(size for _, _, size in chunks)
    return pl.pallas_call(
        functools.partial(_attn_kernel, chunks=tuple(chunks)),
        out_shape=jax.ShapeDtypeStruct((bsz, sq, nh * MLA_V), BF16),
        grid=(bsz, nh // 2, sq // tq),
        in_specs=in_specs,
        out_specs=pl.BlockSpec((1, tq, LANE), lambda b, h, i: (b, i, h)),
        scratch_shapes=[pltpu.VMEM((2, 2, tq, tk_max), F32)],
        compiler_params=_cparams(("parallel", "parallel", "arbitrary")),
        name="mla_attention",
    )(*args)


def _dot_nt(a, b):
    return lax.dot_general(a, b, (((1,), (1,)), ((), ())), preferred_element_type=F32)


def _dot_tn(a, b):
    return lax.dot_general(a, b, (((0,), (0,)), ((), ())), preferred_element_type=F32)


def _split_bf16(x, n):
    parts = []
    for _ in range(n):
        p = x.astype(BF16)
        parts.append(p)
        x = x - p.astype(F32)
    return parts


def _dot_exact_lhs(a, x):
    hi, lo = _split_bf16(x, 2)
    return (jnp.dot(a, lo, preferred_element_type=F32)
            + jnp.dot(a, hi, preferred_element_type=F32))


_SCAN_COMBOS = (('gla', False), ('gla', True), ('ret', False), ('ret', True))


def _scans_kernel(*refs, nchunk):
    n = len(_SCAN_COMBOS)
    ins, outs, scr = refs[:7 * n], refs[7 * n:9 * n], refs[9 * n:]
    for i, (kind, reverse) in enumerate(_SCAN_COMBOS):
        _scan_core(*ins[7 * i:7 * i + 7], *outs[2 * i:2 * i + 2], scr[i],
                   kind=kind, reverse=reverse, nchunk=nchunk)


def _scan_core(q_ref, k_ref, v_ref, x0_ref, x1_ref, x2_ref, s0_ref, o_ref, sf_ref, st_scr, *,
               kind, reverse, nchunk):
    if kind == 'gla':
        r_ref, w2_ref, bg_ref = x0_ref, x1_ref, x2_ref
    else:
        cos_ref, sin_ref, rd_ref = x0_ref, x1_ref, x2_ref
    step = pl.program_id(1)

    @pl.when(step == 0)
    def _():
        st_scr[...] = s0_ref[0]

    blk = nchunk * CHUNK
    ii = lax.broadcasted_iota(jnp.int32, (blk, blk), 0)
    jj = lax.broadcasted_iota(jnp.int32, (blk, blk), 1)
    same = (ii // CHUNK) == (jj // CHUNK)
    if reverse:
        amask = same & (jj > ii)
    else:
        amask = same & (ii >= jj)
    order = range(nchunk - 1, -1, -1) if reverse else range(nchunk)
    vb = v_ref[...].astype(BF16)

    if kind == 'gla':
        pre = jnp.dot(r_ref[...].astype(BF16), w2_ref[...].astype(BF16),
                      preferred_element_type=F32) + bg_ref[...]
        la = jax.nn.log_sigmoid(pre) * (1.0 / GLA_GATE_NORMALIZER)
        tri = (same & ((jj >= ii) if reverse else (ii >= jj))).astype(BF16)
        cum = _dot_exact_lhs(tri, la)
        totb = jnp.concatenate(
            [jnp.broadcast_to(cum[c * CHUNK + (0 if reverse else CHUNK - 1)][None, :],
                              (CHUNK, cum.shape[1])) for c in range(nchunk)], axis=0)
        k32 = k_ref[...].astype(F32)
        qd = (q_ref[...].astype(F32) * ((GLA_DK ** -0.5) * jnp.exp(cum))).astype(BF16)
        kd = (k32 * jnp.exp(-cum)).astype(BF16)
        kz = (k32 * jnp.exp(totb - cum)).astype(BF16)
        dtot = jnp.exp(totb)
    else:
        pidx = (lax.broadcasted_iota(jnp.int32, (blk, LANE), 0) % CHUNK).astype(F32)
        dd = jnp.where(amask, (jj - ii if reverse else ii - jj).astype(F32), 0.0)
        cs = cos_ref[...]
        sn = sin_ref[...]

    for h in range(4):
        cols = slice(h * LANE, (h + 1) * LANE)
        vh = vb[:, cols]
        if kind == 'gla':
            qh, kzh = qd[:, cols], kz[:, cols]
            att = jnp.where(amask, _dot_nt(qh, kd[:, cols]), 0.0)
            xi = None
        else:
            lg = -jnp.exp(rd_ref[h:h + 1, :])
            q = q_ref[:, cols].astype(F32)
            k = k_ref[:, cols].astype(F32)
            qh = (q * cs + pltpu.roll(q, LANE // 2, 1) * sn).astype(BF16)
            kr = (k * cs + pltpu.roll(k, LANE // 2, 1) * sn) * (RET_DK ** -0.5)
            if reverse:
                zeta = jnp.exp(pidx * lg)
                xi = jnp.exp((CHUNK - pidx) * lg)
            else:
                zeta = jnp.exp((CHUNK - 1 - pidx) * lg)
                xi = jnp.exp((pidx + 1.0) * lg)
            kzh = (kr * zeta).astype(BF16)
            lgb = jnp.concatenate([lg] * (blk // LANE), axis=1) if blk > LANE else lg[:, :blk]
            att = _dot_nt(qh, kr.astype(BF16)) * jnp.where(amask, jnp.exp(dd * lgb), 0.0)
            gch = jnp.exp(CHUNK * lg)
        o_intra = jnp.dot(att.astype(BF16), vh, preferred_element_type=F32)
        st = st_scr[h]
        for c in order:
            rows = slice(c * CHUNK, (c + 1) * CHUNK)
            o = o_intra[rows] + (_dot_nt(qh[rows], st.astype(BF16)) if xi is None
                                 else _dot_nt(qh[rows], st.astype(BF16)) * xi[rows])
            decay = dtot[c * CHUNK:c * CHUNK + 1, cols] if kind == 'gla' else gch
            st = st * decay + _dot_tn(vh[rows], kzh[rows])
            o_ref[rows, cols] = o
        st_scr[h] = st

    @pl.when(step == pl.num_programs(1) - 1)
    def _():
        sf_ref[0] = st_scr[...]


def _scans(mg, lat, bsz, seq, states, gla_extra, ret_extra):
    blk = _pick(seq, 256, CHUNK)
    nb = seq // blk
    const = lambda b, s: (0, 0)
    sspec = pl.BlockSpec((1, 4, LANE, LANE), lambda b, s: (b, 0, 0, 0))
    in_specs, args, out_specs = [], [], []
    for (kind, reverse), s0 in zip(_SCAN_COMBOS, states):
        pos = (lambda s: nb - 1 - s) if reverse else (lambda s: s)
        row = lambda b, s, pos=pos: b * nb + pos(s)
        col0 = 0 if kind == 'gla' else 4
        in_specs += [pl.BlockSpec((blk, 512), lambda b, s, c=col0 + j, row=row: (row(b, s), c))
                     for j in range(3)]
        args += [mg, mg, mg]
        if kind == 'gla':
            in_specs += [pl.BlockSpec((blk, LANE), lambda b, s, row=row: (row(b, s), 3)),
                         pl.BlockSpec((LANE, 512), const), pl.BlockSpec((1, 512), const)]
            args += [lat, *gla_extra[reverse]]
        else:
            in_specs += [pl.BlockSpec((blk, LANE), lambda b, s, pos=pos: (pos(s), 0)),
                         pl.BlockSpec((blk, LANE), lambda b, s, pos=pos: (pos(s), 0)),
                         pl.BlockSpec((SUBLANE, LANE), const)]
            args += [ret_extra[0], ret_extra[1], ret_extra[2][reverse]]
        in_specs.append(sspec)
        args.append(s0)
        out_specs += [pl.BlockSpec((blk, 512), lambda b, s, row=row: (row(b, s), 0)), sspec]
    r = bsz * seq
    n = len(_SCAN_COMBOS)
    outs = pl.pallas_call(
        functools.partial(_scans_kernel, nchunk=blk // CHUNK),
        out_shape=(jax.ShapeDtypeStruct((r, 512), F32),
                   jax.ShapeDtypeStruct((bsz, 4, LANE, LANE), F32)) * n,
        grid=(bsz, nb),
        in_specs=in_specs,
        out_specs=out_specs,
        scratch_shapes=[pltpu.VMEM((4, LANE, LANE), F32)] * n,
        compiler_params=_cparams(("parallel", "arbitrary")),
        name="recurrent_scans",
    )(*args)
    return list(outs[0::2]), list(outs[1::2])


def _merge_kernel(ym_ref, ogf_ref, ogb_ref, orf_ref, orb_ref, gg_ref, gr_ref, g0_ref, g1_ref,
                  g2_ref, on_ref, bg_ref, wb_ref, wo_ref, h_ref, gt_ref, o_ref):
    def gated_head_norm(of_ref, ob_ref, g_ref, w):
        parts = []
        for hd in range(4):
            cols = slice(hd * LANE, (hd + 1) * LANE)
            osum = of_ref[:, cols] + ob_ref[:, cols]
            y = osum * _rms(osum, LANE)
            if w is not None:
                y = y * w
            g = g_ref[:, cols].astype(F32)
            parts.append((y * (g * jax.nn.sigmoid(g))).astype(BF16))
        return jnp.concatenate(parts, axis=1)

    ys = (ym_ref[...], gated_head_norm(ogf_ref, ogb_ref, gg_ref, on_ref[...]),
          gated_head_norm(orf_ref, orb_ref, gr_ref, None))
    u = None
    for n, (y, g_ref) in enumerate(zip(ys, (g0_ref, g1_ref, g2_ref))):
        cols = slice(n * D_MODEL, (n + 1) * D_MODEL)
        t = jnp.dot(y, wb_ref[n], preferred_element_type=F32)
        t = jax.nn.sigmoid(g_ref[...].astype(F32) + bg_ref[:, cols]) * t
        u = t if u is None else u + t
    out = jnp.dot(u.astype(BF16), wo_ref[...], preferred_element_type=F32)
    o_ref[...] = h_ref[...] + gt_ref[0] * out


def _merge(y_mla, scan_o, mg, on, bg, wb, wo, h, gt):
    r, d = h.shape
    g = gt.shape[0]
    rpg = r // g
    tm = _pick(rpg, 512)
    yspec = pl.BlockSpec((tm, BRANCH_W), lambda i: (i, 0))
    gspec = lambda c: pl.BlockSpec((tm, BRANCH_W), lambda i, c=c: (i, c))
    mspec = lambda c: pl.BlockSpec((tm, d), lambda i, c=c: (i, c))
    return pl.pallas_call(
        _merge_kernel,
        out_shape=jax.ShapeDtypeStruct((r, d), F32),
        grid=(r // tm,),
        in_specs=[yspec] * 5 + [gspec(3), gspec(7), mspec(4), mspec(5), mspec(6),
                  pl.BlockSpec((1, LANE), lambda i: (0, 0)),
                  pl.BlockSpec((1, GATE_W), lambda i: (0, 0)),
                  pl.BlockSpec((N_BRANCH, BRANCH_W, d), lambda i: (0, 0, 0)),
                  pl.BlockSpec((d, d), lambda i: (0, 0)),
                  pl.BlockSpec((tm, d), lambda i: (i, 0)),
                  pl.BlockSpec((1, 1, d), lambda i: (i * tm // rpg, 0, 0))],
        out_specs=pl.BlockSpec((tm, d), lambda i: (i, 0)),
        compiler_params=_cparams(("parallel",)),
        name="gated_merge",
    )(y_mla, *scan_o, mg, mg, mg, mg, mg, on, bg, wb, wo, h, gt)


def _ffn_kernel(hp_ref, h_ref, hn_ref, sh_ref, sc_ref, gt_ref, nw_ref, wg_ref, wu_ref, wdw_ref,
                bdw_ref, wo_ref, o_ref, *, tm, seq, tf):
    i = pl.program_id(0)

    def norm_mod(x):
        y = x * lax.rsqrt(jnp.mean(x * x, axis=-1, keepdims=True) + EPS) * nw_ref[...]
        return y * (1.0 + sc_ref[0]) + sh_ref[0]

    row0 = i * tm
    prev_ok = (row0 % seq != 0).astype(F32)
    next_ok = ((row0 + tm) % seq != 0).astype(F32)
    a_mid = norm_mod(h_ref[...]).astype(BF16)
    a_ext = jnp.concatenate([(norm_mod(hp_ref[...]) * prev_ok).astype(BF16), a_mid,
                             (norm_mod(hn_ref[...]) * next_ok).astype(BF16)], axis=0)
    acc = None
    for j in range(wg_ref.shape[1] // tf):
        cs = slice(j * tf, (j + 1) * tf)
        gate = jnp.dot(a_ext, wg_ref[:, cs], preferred_element_type=F32)
        up = jnp.dot(a_mid, wu_ref[:, cs], preferred_element_type=F32)
        g_prev = pltpu.roll(gate, 1, 0)[HALO:HALO + tm]
        g_next = pltpu.roll(gate, tm + 2 * HALO - 1, 0)[HALO:HALO + tm]
        conv = (g_prev * wdw_ref[0:1, cs] + gate[HALO:HALO + tm] * wdw_ref[1:2, cs]
                + g_next * wdw_ref[2:3, cs] + bdw_ref[:, cs])
        act = (jax.nn.gelu(conv, approximate=True) * up).astype(BF16)
        part = jnp.dot(act, wo_ref[cs, :], preferred_element_type=F32)
        acc = part if acc is None else acc + part
    o_ref[...] = h_ref[...] + gt_ref[0] * acc


def _ffn(h, shift, scale, gt, nw, wg, wu, wdw, bdw, wo, seq):
    r, d = h.shape
    g = shift.shape[0]
    rpg = r // g
    dff = wg.shape[1]
    tm = _pick(seq, 512, HALO)
    tf = _pick(dff, 2 * LANE, LANE)
    nsub = tm // HALO
    nrb = r // HALO
    gidx = lambda i: (i * tm // rpg, 0, 0)
    const = lambda i: (0, 0)
    kern = functools.partial(_ffn_kernel, tm=tm, seq=seq, tf=tf)
    return pl.pallas_call(
        kern,
        out_shape=jax.ShapeDtypeStruct((r, d), F32),
        grid=(r // tm,),
        in_specs=[pl.BlockSpec((HALO, d), lambda i: (jnp.maximum(i * nsub - 1, 0), 0)),
                  pl.BlockSpec((tm, d), lambda i: (i, 0)),
                  pl.BlockSpec((HALO, d), lambda i: (jnp.minimum((i + 1) * nsub, nrb - 1), 0)),
                  pl.BlockSpec((1, 1, d), gidx), pl.BlockSpec((1, 1, d), gidx),
                  pl.BlockSpec((1, 1, d), gidx),
                  pl.BlockSpec((1, d), const),
                  pl.BlockSpec((d, dff), const),
                  pl.BlockSpec((d, dff), const),
                  pl.BlockSpec((3, dff), const),
                  pl.BlockSpec((1, dff), const),
                  pl.BlockSpec((dff, d), const)],
        out_specs=pl.BlockSpec((tm, d), lambda i: (i, 0)),
        compiler_params=_cparams(("parallel",)),
        name="conv_ffn",
    )(h, h, h, shift, scale, gt, nw, wg, wu, wdw, bdw, wo)


def _mla_weight_layout():
    pidx, psgn = _rope_partner(MLA_ROPE)
    qa_idx, qa_sgn, qb_idx, qb_sgn = [], [], [], []
    k_idx, k_sgn, v_idx, v_sgn = [], [], [], []
    for h in range(MLA_HEADS):
        base = h * MLA_QK
        qa_idx += list(range(base, base + MLA_QK)) + [0] * 32
        qa_sgn += [1.0] * MLA_QK + [0.0] * 32
        qb_idx += [0] * 64 + (base + MLA_NOPE + pidx).tolist() + [0] * 32
        qb_sgn += [0.0] * 64 + psgn.tolist() + [0.0] * 32
        kb = h * (MLA_NOPE + MLA_V)
        k_idx += list(range(kb, kb + MLA_NOPE)) + [0] * 64
        k_sgn += [1.0] * 64 + [0.0] * 64
        vcols = list(range(kb + MLA_NOPE, kb + MLA_NOPE + MLA_V))
        if h % 2 == 0:
            v_idx += vcols + [0] * 64
            v_sgn += [1.0] * 64 + [0.0] * 64
        else:
            v_idx += [0] * 64 + vcols
            v_sgn += [0.0] * 64 + [1.0] * 64
    f = lambda a, t: np.asarray(a, t)
    return (f(qa_idx + qb_idx, np.int32), f(qa_sgn + qb_sgn, np.float32),
            f(k_idx, np.int32), f(k_sgn, np.float32), f(v_idx, np.int32), f(v_sgn, np.float32))


def _slot_vec(w):
    pidx, _ = _rope_partner(MLA_ROPE)
    z32 = jnp.zeros((32,), F32)
    a = jnp.concatenate([w, z32])
    p = jnp.concatenate([jnp.zeros((64,), F32), w[MLA_NOPE + pidx], z32])
    return a.reshape(1, LANE), p.reshape(1, LANE)


def _mla_tables(seq, rope):
    ones = jnp.ones((seq, 64), F32)
    if not rope:
        return jnp.ones((seq, LANE), F32), jnp.zeros((seq, LANE), F32)
    pos = jnp.arange(seq)
    dim = MLA_ROPE // 2
    inv = ROPE_THETA ** (-jnp.arange(dim // 2, dtype=F32) * 2.0 / dim)
    ar = (pos // GRID_W).astype(F32)[:, None] * inv[None, :]
    ac = (pos % GRID_W).astype(F32)[:, None] * inv[None, :]
    cos = jnp.concatenate([ones, jnp.cos(ar), jnp.cos(ar), jnp.cos(ac), jnp.cos(ac),
                           jnp.ones((seq, 32), F32)], axis=1)
    sin = jnp.concatenate([0.0 * ones, jnp.sin(ar), jnp.sin(ar), jnp.sin(ac), jnp.sin(ac),
                           jnp.zeros((seq, 32), F32)], axis=1)
    return cos, sin


def _ret_tables(start, seq):
    inv = 1.0 / (RET_THETA ** jnp.linspace(0.0, 1.0, RET_DK // 2, dtype=F32))
    ang = (start + jnp.arange(seq)).astype(F32)[:, None] * inv[None, :]
    cos, sin = jnp.cos(ang), jnp.sin(ang)
    return jnp.concatenate([cos, cos], axis=1), jnp.concatenate([-sin, sin], axis=1)


def _split_in_proj(w):
    o = _IN_OFF
    z = lambda n: jnp.zeros((w.shape[0], n), w.dtype)
    kr = o['mla_kr']
    q8 = MLA_ROPE // 4
    partner = []
    for base in (kr, kr + 2 * q8):
        partner += [-w[:, base + q8:base + 2 * q8], w[:, base:base + q8]]
    w_lat = jnp.concatenate(
        [w[:, :kr], z(64), w[:, kr:kr + MLA_ROPE], w[:, o['gla_rf']:o['gla_rf'] + 2 * GLA_GATE_RANK],
         z(64)] + partner + [z(32)], axis=1)
    w_mg = jnp.concatenate([w[:, o['gla_q']:o['gla_rf']], w[:, o['ret_q']:]], axis=1)
    return w_lat.astype(BF16), w_mg.astype(BF16)


def _mixers(lat, mg, bsz, seq, lw, mla_tabs, ret_tabs, states):
    qkv = _mla_prep(lat, mla_tabs[0], mla_tabs[1], lw['qna'], lw['kvna'], lw['wq2'], lw['wk'],
                    lw['wv'], lw['qn'], lw['qnp'], lw['kn'], lw['knp'], bsz, seq)
    scan_o, finals = _scans(mg, lat, bsz, seq, states,
                            {False: (lw['w2f'], lw['bgf']), True: (lw['w2b'], lw['bgb'])},
                            (ret_tabs[0], ret_tabs[1], {False: lw['rd_f'], True: lw['rd_b']}))
    return qkv, scan_o, finals


def kernel(x, c, ctx, c_ctx, w_ada, b_ada, norm1_w, norm2_w, w_in, b_gate, mla_q_norm_a, mla_w_qb,
           mla_kv_norm_a, mla_w_kvb, mla_q_norm, mla_k_norm, gla_w_gk2, gla_b_gk, gla_o_norm,
           ret_decay, w_branch, w_out, w_ffn_in, w_dw, b_dw, w_ffn_out):
    bsz, seq, d = x.shape
    clen = ctx.shape[1]
    depth = w_ada.shape[0]
    r_lat, r_ctx = bsz * seq, bsz * clen

    npad = -(bsz + 1) % SUBLANE
    cc = jnp.concatenate([c, c_ctx[None, :], jnp.zeros((npad, d), F32)], axis=0)
    mod_all = _ada_mod(cc, w_ada, b_ada)

    qidx, qsgn, kidx, ksgn, vidx, vsgn = _mla_weight_layout()
    mla_lat_tabs = _mla_tables(seq, True)
    mla_ctx_tabs = _mla_tables(clen, False)
    ret_lat_tabs = _ret_tables(clen, seq)
    ret_ctx_tabs = _ret_tables(0, clen)
    zstate = jnp.zeros((bsz, 4, LANE, LANE), F32)

    h = x.reshape(r_lat, d)
    hc = ctx.reshape(r_ctx, d)
    for l in range(depth):
        need_ctx = l < depth - 1
        mods = mod_all[l].reshape(-1, 6, d)
        ml = [mods[:bsz, j].reshape(bsz, 1, d) for j in range(6)]
        mc = [mods[bsz:bsz + 1, j].reshape(1, 1, d) for j in range(6)]

        w_lat, w_mg = _split_in_proj(w_in[l])
        qn, qnp = _slot_vec(mla_q_norm[l])
        kn, knp = _slot_vec(mla_k_norm[l])
        w2 = gla_w_gk2[l]
        zr = jnp.zeros((LANE - 2 * GLA_GATE_RANK, 4 * GLA_DK), F32)
        z16 = jnp.zeros((GLA_GATE_RANK, 4 * GLA_DK), F32)
        lw = dict(
            qna=mla_q_norm_a[l].reshape(1, -1), kvna=mla_kv_norm_a[l].reshape(1, -1),
            wq2=(mla_w_qb[l][:, qidx] * qsgn[None, :]).astype(BF16),
            wk=(mla_w_kvb[l][:, kidx] * ksgn[None, :]).astype(BF16),
            wv=(mla_w_kvb[l][:, vidx] * vsgn[None, :]).astype(BF16),
            qn=qn, qnp=qnp, kn=kn, knp=knp,
            w2f=jnp.concatenate([zr, w2[0], z16], axis=0),
            w2b=jnp.concatenate([zr, z16, w2[1]], axis=0),
            bgf=gla_b_gk[l, 0].reshape(1, -1), bgb=gla_b_gk[l, 1].reshape(1, -1),
            gla_on=gla_o_norm[l].reshape(1, LANE),
            rd_f=jnp.broadcast_to(jnp.pad(ret_decay[l, 0], (0, 4))[:, None], (SUBLANE, LANE)),
            rd_b=jnp.broadcast_to(jnp.pad(ret_decay[l, 1], (0, 4))[:, None], (SUBLANE, LANE)),
        )
        nw1 = norm1_w[l].reshape(1, d)
        nw2 = norm2_w[l].reshape(1, d)
        bg = b_gate[l].reshape(1, GATE_W)
        wb = w_branch[l].astype(BF16)
        wo = w_out[l].astype(BF16)
        wg = w_ffn_in[l][:, :D_FF].astype(BF16)
        wu = w_ffn_in[l][:, D_FF:].astype(BF16)
        wfo = w_ffn_out[l].astype(BF16)
        bdw = b_dw[l].reshape(1, D_FF)

        def project(hh, m, name):
            lat = _norm_mod_matmul(hh, m[0], m[1], nw1, w_lat, F32, 1024, LAT_W, name + "_lat")
            mg = _norm_mod_matmul(hh, m[0], m[1], nw1, w_mg, BF16, 1024, 1024, name + "_mg")
            return lat, mg

        lat_c, mg_c = project(hc, mc, "in_proj_ctx")
        (q_c, k_c, v_c), so_c, states = _mixers(
            lat_c, mg_c, bsz, clen, lw, mla_ctx_tabs, ret_ctx_tabs, (zstate,) * 4)
        lat_l, mg_l = project(h, ml, "in_proj")
        (q_l, k_l, v_l), so_l, _ = _mixers(
            lat_l, mg_l, bsz, seq, lw, mla_lat_tabs, ret_lat_tabs, states)
        y_mla = _attention(q_l, ((k_c, v_c), (k_l, v_l))).reshape(r_lat, BRANCH_W)
        h = _merge(y_mla, so_l, mg_l, lw['gla_on'], bg, wb, wo, h, ml[2])
        h = _ffn(h, ml[3], ml[4], ml[5], nw2, wg, wu, w_dw[l], bdw, wfo, seq)
        if need_ctx:
            ym_c = _attention(q_c, ((k_c, v_c),)).reshape(r_ctx, BRANCH_W)
            hc = _merge(ym_c, so_c, mg_c, lw['gla_on'], bg, wb, wo, hc, mc[2])
            hc = _ffn(hc, mc[3], mc[4], mc[5], nw2, wg, wu, w_dw[l], bdw, wfo, clen)
    return h.reshape(bsz, seq, d)
```

```python
import functools
import math

import numpy as np
import jax
import jax.numpy as jnp
from jax import lax
from jax.experimental import pallas as pl
from jax.experimental.pallas import tpu as pltpu

F32 = jnp.float32
BF16 = jnp.bfloat16

D_MODEL = 1024
GRID_W = 64
N_BRANCH = 3
BRANCH_W = 512
MLA_HEADS = 8
MLA_NOPE = 64
MLA_ROPE = 32
MLA_QK = MLA_NOPE + MLA_ROPE
MLA_V = BRANCH_W // MLA_HEADS
MLA_Q_LORA = 256
MLA_KV_LORA = 128
GLA_HEADS = 4
GLA_DK = 128
GLA_DV = 128
GLA_GATE_RANK = 16
GLA_GATE_NORMALIZER = 16.0
RET_HEADS = 4
RET_DK = 128
RET_DV = 128
D_FF = 2816
CHUNK = 64
ROPE_THETA = 10000.0
RET_THETA = 10000.0
EPS = 1e-6
LOG2E = math.log2(math.e)
LANE = 128
SUBLANE = 8
HALO = 16
VMEM_LIMIT = 56 * 1024 * 1024

_IN_WIDTHS = (('mla_q', 256), ('mla_kv', 128), ('mla_kr', 32), ('gla_q', 512), ('gla_k', 512),
              ('gla_v', 512), ('gla_g', 512), ('gla_rf', 16), ('gla_rb', 16), ('ret_q', 512),
              ('ret_k', 512), ('ret_v', 512), ('ret_g', 512), ('gate_mla', 1024),
              ('gate_gla', 1024), ('gate_ret', 1024))
_IN_OFF = {}
_o = 0
for _n, _w in _IN_WIDTHS:
    _IN_OFF[_n] = _o
    _o += _w
N_IN = _o
LAT_W = 5 * LANE
MIX_W = 8 * 512
GATE_W = 3 * D_MODEL


def _cparams(sem):
    return pltpu.CompilerParams(dimension_semantics=sem, vmem_limit_bytes=VMEM_LIMIT)


def _pick(n, pref, mult=SUBLANE):
    if n <= pref:
        return n
    for t in range(pref - pref % mult, 0, -mult):
        if n % t == 0:
            return t
    return n


def _rope_partner(nrot):
    half = nrot // 2
    q = half // 2
    idx = np.zeros(nrot, np.int32)
    sgn = np.zeros(nrot, np.float32)
    for base in (0, half):
        for i in range(q):
            idx[base + i] = base + i + q
            sgn[base + i] = -1.0
            idx[base + q + i] = base + i
            sgn[base + q + i] = 1.0
    return idx, sgn


def _ada_kernel(c_ref, w_ref, b_ref, o_ref):
    c = c_ref[...]
    cs = c * jax.nn.sigmoid(c)
    o_ref[0] = jnp.dot(cs, w_ref[0], preferred_element_type=F32,
                       precision=lax.Precision.HIGHEST) + b_ref[0]


def _ada_mod(cc, w_ada, b_ada):
    nl, d, n = w_ada.shape
    tn = _pick(n, 1536, LANE)
    return pl.pallas_call(
        _ada_kernel,
        out_shape=jax.ShapeDtypeStruct((nl, cc.shape[0], n), F32),
        grid=(nl, n // tn),
        in_specs=[pl.BlockSpec((cc.shape[0], d), lambda l, j: (0, 0)),
                  pl.BlockSpec((1, d, tn), lambda l, j: (l, 0, j)),
                  pl.BlockSpec((1, 1, tn), lambda l, j: (l, 0, j))],
        out_specs=pl.BlockSpec((1, cc.shape[0], tn), lambda l, j: (l, 0, j)),
        compiler_params=_cparams(("arbitrary", "arbitrary")),
        name="ada_mod",
    )(cc, w_ada, b_ada.reshape(nl, 1, n))


def _nmm_kernel(x_ref, sh_ref, sc_ref, nw_ref, w_ref, o_ref, a_scr):
    @pl.when(pl.program_id(1) == 0)
    def _():
        x = x_ref[...]
        y = x * lax.rsqrt(jnp.mean(x * x, axis=-1, keepdims=True) + EPS) * nw_ref[...]
        a_scr[...] = (y * (1.0 + sc_ref[0]) + sh_ref[0]).astype(BF16)

    o_ref[...] = jnp.dot(a_scr[...], w_ref[...], preferred_element_type=F32).astype(o_ref.dtype)


def _norm_mod_matmul(x, shift, scale, nw, w, out_dtype, tm, tn, name):
    r, d = x.shape
    g = shift.shape[0]
    rpg = r // g
    n = w.shape[1]
    tm = _pick(rpg, tm)
    tn = _pick(n, tn, LANE)
    return pl.pallas_call(
        _nmm_kernel,
        out_shape=jax.ShapeDtypeStruct((r, n), out_dtype),
        grid=(r // tm, n // tn),
        in_specs=[pl.BlockSpec((tm, d), lambda i, j: (i, 0)),
                  pl.BlockSpec((1, 1, d), lambda i, j: (i * tm // rpg, 0, 0)),
                  pl.BlockSpec((1, 1, d), lambda i, j: (i * tm // rpg, 0, 0)),
                  pl.BlockSpec((1, d), lambda i, j: (0, 0)),
                  pl.BlockSpec((d, tn), lambda i, j: (0, j))],
        out_specs=pl.BlockSpec((tm, tn), lambda i, j: (i, j)),
        scratch_shapes=[pltpu.VMEM((tm, d), BF16)],
        compiler_params=_cparams(("parallel", "arbitrary")),
        name=name,
    )(x, shift, scale, nw, w)


def _rms(x, n):
    return lax.rsqrt(jnp.sum(x * x, axis=-1, keepdims=True) * (1.0 / n) + EPS)


def _mla_prep_kernel(cq_ref, ckv_ref, kr_ref, krp_ref, cos_ref, sin_ref, qna_ref, kvna_ref,
                     wq_ref, wk_ref, wv_ref, qn_ref, qnp_ref, kn_ref, knp_ref,
                     q_out, k_out, v_out):
    cos = cos_ref[...]
    sin = sin_ref[...]
    cq = cq_ref[...]
    cqn = (cq * _rms(cq, MLA_Q_LORA) * qna_ref[...]).astype(BF16)
    qx = jnp.dot(cqn, wq_ref[...], preferred_element_type=F32)
    ckv = ckv_ref[...]
    ckvn = (ckv * _rms(ckv, MLA_KV_LORA) * kvna_ref[...]).astype(BF16)
    kx = jnp.dot(ckvn, wk_ref[...], preferred_element_type=F32)
    vx = jnp.dot(ckvn, wv_ref[...], preferred_element_type=F32)
    lane = lax.broadcasted_iota(jnp.int32, kr_ref.shape, 1)
    krm = jnp.where(lane < MLA_QK, kr_ref[...], 0.0)
    krp = krp_ref[...]
    ssq_kr = jnp.sum(krm * krm, axis=-1, keepdims=True)
    qc = qn_ref[...] * cos
    qs = qnp_ref[...] * sin
    kc = kn_ref[...] * cos
    ks = knp_ref[...] * sin
    hw = MLA_HEADS * LANE
    for h in range(MLA_HEADS):
        sl = slice(h * LANE, (h + 1) * LANE)
        xa = qx[:, sl]
        xb = qx[:, hw + h * LANE: hw + (h + 1) * LANE]
        r = _rms(xa, MLA_QK) * (MLA_QK ** -0.5 * LOG2E)
        q_out[0, h] = ((xa * qc + xb * qs) * r).astype(BF16)
        kn = kx[:, sl]
        rk = lax.rsqrt((jnp.sum(kn * kn, axis=-1, keepdims=True) + ssq_kr) * (1.0 / MLA_QK) + EPS)
        k_out[0, h] = (((kn + krm) * kc + krp * ks) * rk).astype(BF16)
        ones_lane = MLA_V if h % 2 == 0 else 0
        v_out[0, h] = jnp.where(lane == ones_lane, 1.0, vx[:, sl]).astype(BF16)


def _mla_prep(lat, cos_t, sin_t, qna, kvna, wq2, wk, wv, qn, qnp, kn, knp, bsz, seq):
    tm = _pick(seq, 512)
    nb = seq // tm
    hw = MLA_HEADS * LANE
    row = lambda b, s: b * nb + s
    const = lambda b, s: (0, 0)
    oshape = jax.ShapeDtypeStruct((bsz, MLA_HEADS, seq, LANE), BF16)
    ospec = pl.BlockSpec((1, MLA_HEADS, tm, LANE), lambda b, s: (b, 0, s, 0))
    return pl.pallas_call(
        _mla_prep_kernel,
        out_shape=(oshape, oshape, oshape),
        grid=(bsz, nb),
        in_specs=[pl.BlockSpec((tm, 2 * LANE), lambda b, s: (row(b, s), 0)),
                  pl.BlockSpec((tm, LANE), lambda b, s: (row(b, s), 2)),
                  pl.BlockSpec((tm, LANE), lambda b, s: (row(b, s), 3)),
                  pl.BlockSpec((tm, LANE), lambda b, s: (row(b, s), 4)),
                  pl.BlockSpec((tm, LANE), lambda b, s: (s, 0)),
                  pl.BlockSpec((tm, LANE), lambda b, s: (s, 0)),
                  pl.BlockSpec((1, MLA_Q_LORA), const),
                  pl.BlockSpec((1, MLA_KV_LORA), const),
                  pl.BlockSpec((MLA_Q_LORA, 2 * hw), const),
                  pl.BlockSpec((MLA_KV_LORA, hw), const),
                  pl.BlockSpec((MLA_KV_LORA, hw), const),
                  pl.BlockSpec((1, LANE), const), pl.BlockSpec((1, LANE), const),
                  pl.BlockSpec((1, LANE), const), pl.BlockSpec((1, LANE), const)],
        out_specs=(ospec, ospec, ospec),
        compiler_params=_cparams(("parallel", "arbitrary")),
        name="mla_prep",
    )(lat, lat, lat, lat, cos_t, sin_t, qna, kvna, wq2, wk, wv, qn, qnp, kn, knp)


def _attn_kernel(*refs, chunks):
    q_ref, kv, o_ref, s_scr = refs[0], refs[1:-2], refs[-2], refs[-1]
    tq = q_ref.shape[2]
    accs = []
    for hh in range(2):
        q = q_ref[0, hh]

        def scores(c, hh=hh, q=q):
            src, start, size = chunks[c]
            s = _dot_nt(q, kv[2 * src][0, hh, start:start + size, :])
            s_scr[hh, c % 2, :, :size] = s
            return jnp.max(s, axis=-1, keepdims=True)

        m = jnp.full((tq, 1), -jnp.inf, F32)
        acc = jnp.zeros((tq, LANE), F32)
        mt = scores(0)
        for c, (src, start, size) in enumerate(chunks):
            mt_next = scores(c + 1) if c + 1 < len(chunks) else None
            m_new = jnp.maximum(m, mt)
            p = jnp.exp2(s_scr[hh, c % 2, :, :size] - m_new)
            acc = jnp.exp2(m - m_new) * acc + jnp.dot(
                p.astype(BF16), kv[2 * src + 1][0, hh, start:start + size, :],
                preferred_element_type=F32)
            m, mt = m_new, mt_next
        accs.append(acc)
    lane = lax.broadcasted_iota(jnp.int32, (tq, LANE), 1)
    inv0 = 1.0 / accs[0][:, MLA_V:MLA_V + 1]
    inv1 = 1.0 / accs[1][:, 0:1]
    o_ref[0] = jnp.where(lane < MLA_V, accs[0] * inv0, accs[1] * inv1).astype(o_ref.dtype)


def _attention(q, kvs):
    bsz, nh, sq, _ = q.shape
    tq = _pick(sq, 512)
    chunks, in_specs, args = [], [pl.BlockSpec((1, 2, tq, LANE), lambda b, h, i: (b, h, i, 0))], [q]
    for src, (k, v) in enumerate(kvs):
        sk = k.shape[2]
        tk = next(t for t in (1024, 512, 256, 128, sk) if sk % t == 0)
        chunks += [(src, start, tk) for start in range(0, sk, tk)]
        in_specs += [pl.BlockSpec((1, 2, sk, LANE), lambda b, h, i: (b, h, 0, 0))] * 2
        args += [k, v]
    tk_max = max(size for _, _, size in chunks)
    return pl.pallas_call(
        functools.partial(_attn_kernel, chunks=tuple(chunks)),
        out_shape=jax.ShapeDtypeStruct((bsz, sq, nh * MLA_V), BF16),
        grid=(bsz, nh // 2, sq // tq),
        in_specs=in_specs,
        out_specs=pl.BlockSpec((1, tq, LANE), lambda b, h, i: (b, i, h)),
        scratch_shapes=[pltpu.VMEM((2, 2, tq, tk_max), F32)],
        compiler_params=_cparams(("parallel", "parallel", "arbitrary")),
        name="mla_attention",
    )(*args)


def _dot_nt(a, b):
    return lax.dot_general(a, b, (((1,), (1,)), ((), ())), preferred_element_type=F32)


def _dot_tn(a, b):
    return lax.dot_general(a, b, (((0,), (0,)), ((), ())), preferred_element_type=F32)


def _split_bf16(x, n):
    parts = []
    for _ in range(n):
        p = x.astype(BF16)
        parts.append(p)
        x = x - p.astype(F32)
    return parts


def _dot_exact_lhs(a, x):
    hi, lo = _split_bf16(x, 2)
    return (jnp.dot(a, lo, preferred_element_type=F32)
            + jnp.dot(a, hi, preferred_element_type=F32))


_SCAN_COMBOS = (('gla', False), ('gla', True), ('ret', False), ('ret', True))


def _scans_kernel(*refs, nchunk):
    n = len(_SCAN_COMBOS)
    ins, outs, scr = refs[:7 * n], refs[7 * n:9 * n], refs[9 * n:]
    step = pl.program_id(1)

    @pl.when(step == 0)
    def _():
        for i in range(n):
            scr[i][...] = ins[7 * i + 6][0]

    chains = [_scan_chain(*ins[7 * i:7 * i + 6], outs[2 * i], scr[i],
                          kind=kind, reverse=reverse, nchunk=nchunk)
              for i, (kind, reverse) in enumerate(_SCAN_COMBOS)]
    while chains:
        for chain in list(chains):
            if next(chain, 'done') == 'done':
                chains.remove(chain)

    @pl.when(step == pl.num_programs(1) - 1)
    def _():
        for i in range(n):
            outs[2 * i + 1][0] = scr[i][...]


def _scan_chain(q_ref, k_ref, v_ref, x0_ref, x1_ref, x2_ref, o_ref, st_scr, *,
                kind, reverse, nchunk):
    if kind == 'gla':
        r_ref, w2_ref, bg_ref = x0_ref, x1_ref, x2_ref
    else:
        cos_ref, sin_ref, rd_ref = x0_ref, x1_ref, x2_ref

    blk = nchunk * CHUNK
    ii = lax.broadcasted_iota(jnp.int32, (blk, blk), 0)
    jj = lax.broadcasted_iota(jnp.int32, (blk, blk), 1)
    same = (ii // CHUNK) == (jj // CHUNK)
    if reverse:
        amask = same & (jj > ii)
    else:
        amask = same & (ii >= jj)
    order = range(nchunk - 1, -1, -1) if reverse else range(nchunk)
    vb = v_ref[...].astype(BF16)

    if kind == 'gla':
        pre = jnp.dot(r_ref[...].astype(BF16), w2_ref[...].astype(BF16),
                      preferred_element_type=F32) + bg_ref[...]
        yield
        la = jax.nn.log_sigmoid(pre) * (1.0 / GLA_GATE_NORMALIZER)
        tri = (same & ((jj >= ii) if reverse else (ii >= jj))).astype(BF16)
        cum = _dot_exact_lhs(tri, la)
        yield
        totb = jnp.concatenate(
            [jnp.broadcast_to(cum[c * CHUNK + (0 if reverse else CHUNK - 1)][None, :],
                              (CHUNK, cum.shape[1])) for c in range(nchunk)], axis=0)
        k32 = k_ref[...].astype(F32)
        qd = (q_ref[...].astype(F32) * ((GLA_DK ** -0.5) * jnp.exp(cum))).astype(BF16)
        kd = (k32 * jnp.exp(-cum)).astype(BF16)
        kz = (k32 * jnp.exp(totb - cum)).astype(BF16)
        dtot = jnp.exp(totb)
    else:
        pidx = (lax.broadcasted_iota(jnp.int32, (blk, LANE), 0) % CHUNK).astype(F32)
        dd = jnp.where(amask, (jj - ii if reverse else ii - jj).astype(F32), 0.0)
        cs = cos_ref[...]
        sn = sin_ref[...]

    for h in range(4):
        cols = slice(h * LANE, (h + 1) * LANE)
        vh = vb[:, cols]
        if kind == 'gla':
            qh, kzh = qd[:, cols], kz[:, cols]
            att = jnp.where(amask, _dot_nt(qh, kd[:, cols]), 0.0)
            xi = None
        else:
            lg = -jnp.exp(rd_ref[h:h + 1, :])
            q = q_ref[:, cols].astype(F32)
            k = k_ref[:, cols].astype(F32)
            qh = (q * cs + pltpu.roll(q, LANE // 2, 1) * sn).astype(BF16)
            kr = (k * cs + pltpu.roll(k, LANE // 2, 1) * sn) * (RET_DK ** -0.5)
            if reverse:
                zeta = jnp.exp(pidx * lg)
                xi = jnp.exp((CHUNK - pidx) * lg)
            else:
                zeta = jnp.exp((CHUNK - 1 - pidx) * lg)
                xi = jnp.exp((pidx + 1.0) * lg)
            kzh = (kr * zeta).astype(BF16)
            lgb = jnp.concatenate([lg] * (blk // LANE), axis=1) if blk > LANE else lg[:, :blk]
            att = _dot_nt(qh, kr.astype(BF16)) * jnp.where(amask, jnp.exp(dd * lgb), 0.0)
            gch = jnp.exp(CHUNK * lg)
        yield
        o_intra = jnp.dot(att.astype(BF16), vh, preferred_element_type=F32)
        upd = [_dot_tn(vh[c * CHUNK:(c + 1) * CHUNK], kzh[c * CHUNK:(c + 1) * CHUNK])
               for c in range(nchunk)]
        yield
        st = st_scr[h]
        for c in order:
            rows = slice(c * CHUNK, (c + 1) * CHUNK)
            o = o_intra[rows] + (_dot_nt(qh[rows], st.astype(BF16)) if xi is None
                                 else _dot_nt(qh[rows], st.astype(BF16)) * xi[rows])
            decay = dtot[c * CHUNK:c * CHUNK + 1, cols] if kind == 'gla' else gch
            st = st * decay + upd[c]
            o_ref[rows, cols] = o
        st_scr[h] = st
        yield


def _scans(mg, lat, bsz, seq, states, gla_extra, ret_extra):
    blk = _pick(seq, 256, CHUNK)
    nb = seq // blk
    const = lambda b, s: (0, 0)
    sspec = pl.BlockSpec((1, 4, LANE, LANE), lambda b, s: (b, 0, 0, 0))
    in_specs, args, out_specs = [], [], []
    for (kind, reverse), s0 in zip(_SCAN_COMBOS, states):
        pos = (lambda s: nb - 1 - s) if reverse else (lambda s: s)
        row = lambda b, s, pos=pos: b * nb + pos(s)
        col0 = 0 if kind == 'gla' else 4
        in_specs += [pl.BlockSpec((blk, 512), lambda b, s, c=col0 + j, row=row: (row(b, s), c))
                     for j in range(3)]
        args += [mg, mg, mg]
        if kind == 'gla':
            in_specs += [pl.BlockSpec((blk, LANE), lambda b, s, row=row: (row(b, s), 3)),
                         pl.BlockSpec((LANE, 512), const), pl.BlockSpec((1, 512), const)]
            args += [lat, *gla_extra[reverse]]
        else:
            in_specs += [pl.BlockSpec((blk, LANE), lambda b, s, pos=pos: (pos(s), 0)),
                         pl.BlockSpec((blk, LANE), lambda b, s, pos=pos: (pos(s), 0)),
                         pl.BlockSpec((SUBLANE, LANE), const)]
            args += [ret_extra[0], ret_extra[1], ret_extra[2][reverse]]
        in_specs.append(sspec)
        args.append(s0)
        out_specs += [pl.BlockSpec((blk, 512), lambda b, s, row=row: (row(b, s), 0)), sspec]
    r = bsz * seq
    n = len(_SCAN_COMBOS)
    outs = pl.pallas_call(
        functools.partial(_scans_kernel, nchunk=blk // CHUNK),
        out_shape=(jax.ShapeDtypeStruct((r, 512), F32),
                   jax.ShapeDtypeStruct((bsz, 4, LANE, LANE), F32)) * n,
        grid=(bsz, nb),
        in_specs=in_specs,
        out_specs=out_specs,
        scratch_shapes=[pltpu.VMEM((4, LANE, LANE), F32)] * n,
        compiler_params=_cparams(("parallel", "arbitrary")),
        name="recurrent_scans",
    )(*args)
    return list(outs[0::2]), list(outs[1::2])


def _merge_kernel(ym_ref, ogf_ref, ogb_ref, orf_ref, orb_ref, gg_ref, gr_ref, g0_ref, g1_ref,
                  g2_ref, on_ref, bg_ref, wb_ref, wo_ref, h_ref, gt_ref, o_ref):
    def gated_head_norm(of_ref, ob_ref, g_ref, w, rows):
        parts = []
        for hd in range(4):
            cols = slice(hd * LANE, (hd + 1) * LANE)
            osum = of_ref[rows, cols] + ob_ref[rows, cols]
            y = osum * _rms(osum, LANE)
            if w is not None:
                y = y * w
            g = g_ref[rows, cols].astype(F32)
            parts.append((y * (g * jax.nn.sigmoid(g))).astype(BF16))
        return jnp.concatenate(parts, axis=1)

    def rows_chain(rows):
        ys = (ym_ref[rows, :], gated_head_norm(ogf_ref, ogb_ref, gg_ref, on_ref[...], rows),
              gated_head_norm(orf_ref, orb_ref, gr_ref, None, rows))
        ts = [jnp.dot(y, wb_ref[n], preferred_element_type=F32) for n, y in enumerate(ys)]
        yield
        u = None
        for n, g_ref in enumerate((g0_ref, g1_ref, g2_ref)):
            cols = slice(n * D_MODEL, (n + 1) * D_MODEL)
            t = jax.nn.sigmoid(g_ref[rows, :].astype(F32) + bg_ref[:, cols]) * ts[n]
            u = t if u is None else u + t
        out = jnp.dot(u.astype(BF16), wo_ref[...], preferred_element_type=F32)
        o_ref[rows, :] = h_ref[rows, :] + gt_ref[0] * out
        yield

    tm = h_ref.shape[0]
    sub = tm // 2 if tm % (2 * HALO) == 0 else tm
    chains = [rows_chain(slice(r0, r0 + sub)) for r0 in range(0, tm, sub)]
    while chains:
        for chain in list(chains):
            if next(chain, 'done') == 'done':
                chains.remove(chain)


def _merge(y_mla, scan_o, mg, on, bg, wb, wo, h, gt):
    r, d = h.shape
    g = gt.shape[0]
    rpg = r // g
    tm = _pick(rpg, 512)
    yspec = pl.BlockSpec((tm, BRANCH_W), lambda i: (i, 0))
    gspec = lambda c: pl.BlockSpec((tm, BRANCH_W), lambda i, c=c: (i, c))
    mspec = lambda c: pl.BlockSpec((tm, d), lambda i, c=c: (i, c))
    return pl.pallas_call(
        _merge_kernel,
        out_shape=jax.ShapeDtypeStruct((r, d), F32),
        grid=(r // tm,),
        in_specs=[yspec] * 5 + [gspec(3), gspec(7), mspec(4), mspec(5), mspec(6),
                  pl.BlockSpec((1, LANE), lambda i: (0, 0)),
                  pl.BlockSpec((1, GATE_W), lambda i: (0, 0)),
                  pl.BlockSpec((N_BRANCH, BRANCH_W, d), lambda i: (0, 0, 0)),
                  pl.BlockSpec((d, d), lambda i: (0, 0)),
                  pl.BlockSpec((tm, d), lambda i: (i, 0)),
                  pl.BlockSpec((1, 1, d), lambda i: (i * tm // rpg, 0, 0))],
        out_specs=pl.BlockSpec((tm, d), lambda i: (i, 0)),
        compiler_params=_cparams(("parallel",)),
        name="gated_merge",
    )(y_mla, *scan_o, mg, mg, mg, mg, mg, on, bg, wb, wo, h, gt)


def _ffn_kernel(hp_ref, h_ref, hn_ref, sh_ref, sc_ref, gt_ref, nw_ref, wg_ref, wu_ref, wdw_ref,
                bdw_ref, wo_ref, o_ref, *, tm, seq, tf):
    i = pl.program_id(0)

    def norm_mod(x):
        y = x * lax.rsqrt(jnp.mean(x * x, axis=-1, keepdims=True) + EPS) * nw_ref[...]
        return y * (1.0 + sc_ref[0]) + sh_ref[0]

    row0 = i * tm
    prev_ok = (row0 % seq != 0).astype(F32)
    next_ok = ((row0 + tm) % seq != 0).astype(F32)

    a_mid = norm_mod(h_ref[...]).astype(BF16)
    a_ext = jnp.concatenate([(norm_mod(hp_ref[...]) * prev_ok).astype(BF16), a_mid,
                             (norm_mod(hn_ref[...]) * next_ok).astype(BF16)], axis=0)
    nchunk = wg_ref.shape[1] // tf

    def hidden(j):
        cs = slice(j * tf, (j + 1) * tf)
        gate = jnp.dot(a_ext, wg_ref[:, cs], preferred_element_type=F32)
        up = jnp.dot(a_mid, wu_ref[:, cs], preferred_element_type=F32)
        return gate, up

    acc = None
    gate, up = hidden(0)
    for j in range(nchunk):
        cs = slice(j * tf, (j + 1) * tf)
        nxt = hidden(j + 1) if j + 1 < nchunk else None
        g_prev = pltpu.roll(gate, 1, 0)[HALO:HALO + tm]
        g_next = pltpu.roll(gate, tm + 2 * HALO - 1, 0)[HALO:HALO + tm]
        conv = (g_prev * wdw_ref[0:1, cs] + gate[HALO:HALO + tm] * wdw_ref[1:2, cs]
                + g_next * wdw_ref[2:3, cs] + bdw_ref[:, cs])
        act = (jax.nn.gelu(conv, approximate=True) * up).astype(BF16)
        part = jnp.dot(act, wo_ref[cs, :], preferred_element_type=F32)
        acc = part if acc is None else acc + part
        gate, up = nxt if nxt is not None else (None, None)
    o_ref[...] = h_ref[...] + gt_ref[0] * acc


def _ffn(h, shift, scale, gt, nw, wg, wu, wdw, bdw, wo, seq):
    r, d = h.shape
    g = shift.shape[0]
    rpg = r // g
    dff = wg.shape[1]
    tm = _pick(seq, 512, HALO)
    tf = _pick(dff, 2 * LANE, LANE)
    nsub = tm // HALO
    nrb = r // HALO
    gidx = lambda i: (i * tm // rpg, 0, 0)
    const = lambda i: (0, 0)
    kern = functools.partial(_ffn_kernel, tm=tm, seq=seq, tf=tf)
    return pl.pallas_call(
        kern,
        out_shape=jax.ShapeDtypeStruct((r, d), F32),
        grid=(r // tm,),
        in_specs=[pl.BlockSpec((HALO, d), lambda i: (jnp.maximum(i * nsub - 1, 0), 0)),
                  pl.BlockSpec((tm, d), lambda i: (i, 0)),
                  pl.BlockSpec((HALO, d), lambda i: (jnp.minimum((i + 1) * nsub, nrb - 1), 0)),
                  pl.BlockSpec((1, 1, d), gidx), pl.BlockSpec((1, 1, d), gidx),
                  pl.BlockSpec((1, 1, d), gidx),
                  pl.BlockSpec((1, d), const),
                  pl.BlockSpec((d, dff), const),
                  pl.BlockSpec((d, dff), const),
                  pl.BlockSpec((3, dff), const),
                  pl.BlockSpec((1, dff), const),
                  pl.BlockSpec((dff, d), const)],
        out_specs=pl.BlockSpec((tm, d), lambda i: (i, 0)),
        compiler_params=_cparams(("parallel",)),
        name="conv_ffn",
    )(h, h, h, shift, scale, gt, nw, wg, wu, wdw, bdw, wo)


def _mla_weight_layout():
    pidx, psgn = _rope_partner(MLA_ROPE)
    qa_idx, qa_sgn, qb_idx, qb_sgn = [], [], [], []
    k_idx, k_sgn, v_idx, v_sgn = [], [], [], []
    for h in range(MLA_HEADS):
        base = h * MLA_QK
        qa_idx += list(range(base, base + MLA_QK)) + [0] * 32
        qa_sgn += [1.0] * MLA_QK + [0.0] * 32
        qb_idx += [0] * 64 + (base + MLA_NOPE + pidx).tolist() + [0] * 32
        qb_sgn += [0.0] * 64 + psgn.tolist() + [0.0] * 32
        kb = h * (MLA_NOPE + MLA_V)
        k_idx += list(range(kb, kb + MLA_NOPE)) + [0] * 64
        k_sgn += [1.0] * 64 + [0.0] * 64
        vcols = list(range(kb + MLA_NOPE, kb + MLA_NOPE + MLA_V))
        if h % 2 == 0:
            v_idx += vcols + [0] * 64
            v_sgn += [1.0] * 64 + [0.0] * 64
        else:
            v_idx += [0] * 64 + vcols
            v_sgn += [0.0] * 64 + [1.0] * 64
    f = lambda a, t: np.asarray(a, t)
    return (f(qa_idx + qb_idx, np.int32), f(qa_sgn + qb_sgn, np.float32),
            f(k_idx, np.int32), f(k_sgn, np.float32), f(v_idx, np.int32), f(v_sgn, np.float32))


def _slot_vec(w):
    pidx, _ = _rope_partner(MLA_ROPE)
    z32 = jnp.zeros((32,), F32)
    a = jnp.concatenate([w, z32])
    p = jnp.concatenate([jnp.zeros((64,), F32), w[MLA_NOPE + pidx], z32])
    return a.reshape(1, LANE), p.reshape(1, LANE)


def _mla_tables(seq, rope):
    ones = jnp.ones((seq, 64), F32)
    if not rope:
        return jnp.ones((seq, LANE), F32), jnp.zeros((seq, LANE), F32)
    pos = jnp.arange(seq)
    dim = MLA_ROPE // 2
    inv = ROPE_THETA ** (-jnp.arange(dim // 2, dtype=F32) * 2.0 / dim)
    ar = (pos // GRID_W).astype(F32)[:, None] * inv[None, :]
    ac = (pos % GRID_W).astype(F32)[:, None] * inv[None, :]
    cos = jnp.concatenate([ones, jnp.cos(ar), jnp.cos(ar), jnp.cos(ac), jnp.cos(ac),
                           jnp.ones((seq, 32), F32)], axis=1)
    sin = jnp.concatenate([0.0 * ones, jnp.sin(ar), jnp.sin(ar), jnp.sin(ac), jnp.sin(ac),
                           jnp.zeros((seq, 32), F32)], axis=1)
    return cos, sin


def _ret_tables(start, seq):
    inv = 1.0 / (RET_THETA ** jnp.linspace(0.0, 1.0, RET_DK // 2, dtype=F32))
    ang = (start + jnp.arange(seq)).astype(F32)[:, None] * inv[None, :]
    cos, sin = jnp.cos(ang), jnp.sin(ang)
    return jnp.concatenate([cos, cos], axis=1), jnp.concatenate([-sin, sin], axis=1)


def _split_in_proj(w):
    o = _IN_OFF
    z = lambda n: jnp.zeros((w.shape[0], n), w.dtype)
    kr = o['mla_kr']
    q8 = MLA_ROPE // 4
    partner = []
    for base in (kr, kr + 2 * q8):
        partner += [-w[:, base + q8:base + 2 * q8], w[:, base:base + q8]]
    w_lat = jnp.concatenate(
        [w[:, :kr], z(64), w[:, kr:kr + MLA_ROPE], w[:, o['gla_rf']:o['gla_rf'] + 2 * GLA_GATE_RANK],
         z(64)] + partner + [z(32)], axis=1)
    w_mg = jnp.concatenate([w[:, o['gla_q']:o['gla_rf']], w[:, o['ret_q']:]], axis=1)
    return w_lat.astype(BF16), w_mg.astype(BF16)


def _mixers(lat, mg, bsz, seq, lw, mla_tabs, ret_tabs, states):
    qkv = _mla_prep(lat, mla_tabs[0], mla_tabs[1], lw['qna'], lw['kvna'], lw['wq2'], lw['wk'],
                    lw['wv'], lw['qn'], lw['qnp'], lw['kn'], lw['knp'], bsz, seq)
    scan_o, finals = _scans(mg, lat, bsz, seq, states,
                            {False: (lw['w2f'], lw['bgf']), True: (lw['w2b'], lw['bgb'])},
                            (ret_tabs[0], ret_tabs[1], {False: lw['rd_f'], True: lw['rd_b']}))
    return qkv, scan_o, finals


def kernel(x, c, ctx, c_ctx, w_ada, b_ada, norm1_w, norm2_w, w_in, b_gate, mla_q_norm_a, mla_w_qb,
           mla_kv_norm_a, mla_w_kvb, mla_q_norm, mla_k_norm, gla_w_gk2, gla_b_gk, gla_o_norm,
           ret_decay, w_branch, w_out, w_ffn_in, w_dw, b_dw, w_ffn_out):
    bsz, seq, d = x.shape
    clen = ctx.shape[1]
    depth = w_ada.shape[0]
    r_lat, r_ctx = bsz * seq, bsz * clen

    npad = -(bsz + 1) % SUBLANE
    cc = jnp.concatenate([c, c_ctx[None, :], jnp.zeros((npad, d), F32)], axis=0)
    mod_all = _ada_mod(cc, w_ada, b_ada)

    qidx, qsgn, kidx, ksgn, vidx, vsgn = _mla_weight_layout()
    mla_lat_tabs = _mla_tables(seq, True)
    mla_ctx_tabs = _mla_tables(clen, False)
    ret_lat_tabs = _ret_tables(clen, seq)
    ret_ctx_tabs = _ret_tables(0, clen)
    zstate = jnp.zeros((bsz, 4, LANE, LANE), F32)

    h = x.reshape(r_lat, d)
    hc = ctx.reshape(r_ctx, d)
    for l in range(depth):
        need_ctx = l < depth - 1
        mods = mod_all[l].reshape(-1, 6, d)
        ml = [mods[:bsz, j].reshape(bsz, 1, d) for j in range(6)]
        mc = [mods[bsz:bsz + 1, j].reshape(1, 1, d) for j in range(6)]

        w_lat, w_mg = _split_in_proj(w_in[l])
        qn, qnp = _slot_vec(mla_q_norm[l])
        kn, knp = _slot_vec(mla_k_norm[l])
        w2 = gla_w_gk2[l]
        zr = jnp.zeros((LANE - 2 * GLA_GATE_RANK, 4 * GLA_DK), F32)
        z16 = jnp.zeros((GLA_GATE_RANK, 4 * GLA_DK), F32)
        lw = dict(
            qna=mla_q_norm_a[l].reshape(1, -1), kvna=mla_kv_norm_a[l].reshape(1, -1),
            wq2=(mla_w_qb[l][:, qidx] * qsgn[None, :]).astype(BF16),
            wk=(mla_w_kvb[l][:, kidx] * ksgn[None, :]).astype(BF16),
            wv=(mla_w_kvb[l][:, vidx] * vsgn[None, :]).astype(BF16),
            qn=qn, qnp=qnp, kn=kn, knp=knp,
            w2f=jnp.concatenate([zr, w2[0], z16], axis=0),
            w2b=jnp.concatenate([zr, z16, w2[1]], axis=0),
            bgf=gla_b_gk[l, 0].reshape(1, -1), bgb=gla_b_gk[l, 1].reshape(1, -1),
            gla_on=gla_o_norm[l].reshape(1, LANE),
            rd_f=jnp.broadcast_to(jnp.pad(ret_decay[l, 0], (0, 4))[:, None], (SUBLANE, LANE)),
            rd_b=jnp.broadcast_to(jnp.pad(ret_decay[l, 1], (0, 4))[:, None], (SUBLANE, LANE)),
        )
        nw1 = norm1_w[l].reshape(1, d)
        nw2 = norm2_w[l].reshape(1, d)
        bg = b_gate[l].reshape(1, GATE_W)
        wb = w_branch[l].astype(BF16)
        wo = w_out[l].astype(BF16)
        wg = w_ffn_in[l][:, :D_FF].astype(BF16)
        wu = w_ffn_in[l][:, D_FF:].astype(BF16)
        wfo = w_ffn_out[l].astype(BF16)
        bdw = b_dw[l].reshape(1, D_FF)

        def project(hh, m, name):
            lat = _norm_mod_matmul(hh, m[0], m[1], nw1, w_lat, F32, 1024, LAT_W, name + "_lat")
            mg = _norm_mod_matmul(hh, m[0], m[1], nw1, w_mg, BF16, 2048, 1024, name + "_mg")
            return lat, mg

        lat_c, mg_c = project(hc, mc, "in_proj_ctx")
        (q_c, k_c, v_c), so_c, states = _mixers(
            lat_c, mg_c, bsz, clen, lw, mla_ctx_tabs, ret_ctx_tabs, (zstate,) * 4)
        lat_l, mg_l = project(h, ml, "in_proj")
        (q_l, k_l, v_l), so_l, _ = _mixers(
            lat_l, mg_l, bsz, seq, lw, mla_lat_tabs, ret_lat_tabs, states)
        y_mla = _attention(q_l, ((k_c, v_c), (k_l, v_l))).reshape(r_lat, BRANCH_W)
        h = _merge(y_mla, so_l, mg_l, lw['gla_on'], bg, wb, wo, h, ml[2])
        h = _ffn(h, ml[3], ml[4], ml[5], nw2, wg, wu, w_dw[l], bdw, wfo, seq)
        if need_ctx:
            ym_c = _attention(q_c, ((k_c, v_c),)).reshape(r_ctx, BRANCH_W)
            hc = _merge(ym_c, so_c, mg_c, lw['gla_on'], bg, wb, wo, hc, mc[2])
            hc = _ffn(hc, mc[3], mc[4], mc[5], nw2, wg, wu, w_dw[l], bdw, wfo, clen)
    return h.reshape(bsz, seq, d)
```

```python
import functools
import math

import numpy as np
import jax
import jax.numpy as jnp
from jax import lax
from jax.experimental import pallas as pl
from jax.experimental.pallas import tpu as pltpu

F32 = jnp.float32
BF16 = jnp.bfloat16

D_MODEL = 1024
GRID_W = 64
N_BRANCH = 3
BRANCH_W = 512
MLA_HEADS = 8
MLA_NOPE = 64
MLA_ROPE = 32
MLA_QK = MLA_NOPE + MLA_ROPE
MLA_V = BRANCH_W // MLA_HEADS
MLA_Q_LORA = 256
MLA_KV_LORA = 128
GLA_HEADS = 4
GLA_DK = 128
GLA_DV = 128
GLA_GATE_RANK = 16
GLA_GATE_NORMALIZER = 16.0
RET_HEADS = 4
RET_DK = 128
RET_DV = 128
D_FF = 2816
CHUNK = 64
ROPE_THETA = 10000.0
RET_THETA = 10000.0
EPS = 1e-6
LOG2E = math.log2(math.e)
LANE = 128
SUBLANE = 8
HALO = 16
VMEM_LIMIT = 56 * 1024 * 1024

_IN_WIDTHS = (('mla_q', 256), ('mla_kv', 128), ('mla_kr', 32), ('gla_q', 512), ('gla_k', 512),
              ('gla_v', 512), ('gla_g', 512), ('gla_rf', 16), ('gla_rb', 16), ('ret_q', 512),
              ('ret_k', 512), ('ret_v', 512), ('ret_g', 512), ('gate_mla', 1024),
              ('gate_gla', 1024), ('gate_ret', 1024))
_IN_OFF = {}
_o = 0
for _n, _w in _IN_WIDTHS:
    _IN_OFF[_n] = _o
    _o += _w
N_IN = _o
LAT_W = 5 * LANE
MIX_W = 8 * 512
GATE_W = 3 * D_MODEL


def _cparams(sem):
    return pltpu.CompilerParams(dimension_semantics=sem, vmem_limit_bytes=VMEM_LIMIT)


def _pick(n, pref, mult=SUBLANE):
    if n <= pref:
        return n
    for t in range(pref - pref % mult, 0, -mult):
        if n % t == 0:
            return t
    return n


def _rope_partner(nrot):
    half = nrot // 2
    q = half // 2
    idx = np.zeros(nrot, np.int32)
    sgn = np.zeros(nrot, np.float32)
    for base in (0, half):
        for i in range(q):
            idx[base + i] = base + i + q
            sgn[base + i] = -1.0
            idx[base + q + i] = base + i
            sgn[base + q + i] = 1.0
    return idx, sgn


def _ada_kernel(c_ref, w_ref, b_ref, o_ref):
    c = c_ref[...]
    cs = c * jax.nn.sigmoid(c)
    o_ref[0] = jnp.dot(cs, w_ref[0], preferred_element_type=F32,
                       precision=lax.Precision.HIGHEST) + b_ref[0]


def _ada_mod(cc, w_ada, b_ada):
    nl, d, n = w_ada.shape
    tn = _pick(n, 1536, LANE)
    return pl.pallas_call(
        _ada_kernel,
        out_shape=jax.ShapeDtypeStruct((nl, cc.shape[0], n), F32),
        grid=(nl, n // tn),
        in_specs=[pl.BlockSpec((cc.shape[0], d), lambda l, j: (0, 0)),
                  pl.BlockSpec((1, d, tn), lambda l, j: (l, 0, j)),
                  pl.BlockSpec((1, 1, tn), lambda l, j: (l, 0, j))],
        out_specs=pl.BlockSpec((1, cc.shape[0], tn), lambda l, j: (l, 0, j)),
        compiler_params=_cparams(("arbitrary", "arbitrary")),
        name="ada_mod",
    )(cc, w_ada, b_ada.reshape(nl, 1, n))


def _in_proj_kernel(x_ref, sh_ref, sc_ref, nw_ref, wl_ref, w_ref, lat_ref, o_ref, a_scr):
    @pl.when(pl.program_id(1) == 0)
    def _():
        x = x_ref[...]
        y = x * lax.rsqrt(jnp.mean(x * x, axis=-1, keepdims=True) + EPS) * nw_ref[...]
        a = (y * (1.0 + sc_ref[0]) + sh_ref[0]).astype(BF16)
        a_scr[...] = a
        lat_ref[...] = jnp.dot(a, wl_ref[...], preferred_element_type=F32)

    o_ref[...] = jnp.dot(a_scr[...], w_ref[...], preferred_element_type=F32).astype(o_ref.dtype)


def _in_proj(x, shift, scale, nw, w_lat, w_mg, tm, tn, name):
    r, d = x.shape
    g = shift.shape[0]
    rpg = r // g
    n = w_mg.shape[1]
    nl = w_lat.shape[1]
    tm = _pick(rpg, tm)
    tn = _pick(n, tn, LANE)
    return pl.pallas_call(
        _in_proj_kernel,
        out_shape=(jax.ShapeDtypeStruct((r, nl), F32), jax.ShapeDtypeStruct((r, n), BF16)),
        grid=(r // tm, n // tn),
        in_specs=[pl.BlockSpec((tm, d), lambda i, j: (i, 0)),
                  pl.BlockSpec((1, 1, d), lambda i, j: (i * tm // rpg, 0, 0)),
                  pl.BlockSpec((1, 1, d), lambda i, j: (i * tm // rpg, 0, 0)),
                  pl.BlockSpec((1, d), lambda i, j: (0, 0)),
                  pl.BlockSpec((d, nl), lambda i, j: (0, 0)),
                  pl.BlockSpec((d, tn), lambda i, j: (0, j))],
        out_specs=(pl.BlockSpec((tm, nl), lambda i, j: (i, 0)),
                   pl.BlockSpec((tm, tn), lambda i, j: (i, j))),
        scratch_shapes=[pltpu.VMEM((tm, d), BF16)],
        compiler_params=_cparams(("parallel", "arbitrary")),
        name=name,
    )(x, shift, scale, nw, w_lat, w_mg)


def _rms(x, n):
    return lax.rsqrt(jnp.sum(x * x, axis=-1, keepdims=True) * (1.0 / n) + EPS)


def _mla_prep_kernel(cq_ref, ckv_ref, kr_ref, krp_ref, cos_ref, sin_ref, qna_ref, kvna_ref,
                     wq_ref, wk_ref, wv_ref, qn_ref, qnp_ref, kn_ref, knp_ref,
                     q_out, k_out, v_out):
    cos = cos_ref[...]
    sin = sin_ref[...]
    cq = cq_ref[...]
    cqn = (cq * _rms(cq, MLA_Q_LORA) * qna_ref[...]).astype(BF16)
    qx = jnp.dot(cqn, wq_ref[...], preferred_element_type=F32)
    ckv = ckv_ref[...]
    ckvn = (ckv * _rms(ckv, MLA_KV_LORA) * kvna_ref[...]).astype(BF16)
    kx = jnp.dot(ckvn, wk_ref[...], preferred_element_type=F32)
    vx = jnp.dot(ckvn, wv_ref[...], preferred_element_type=F32)
    lane = lax.broadcasted_iota(jnp.int32, kr_ref.shape, 1)
    krm = jnp.where(lane < MLA_QK, kr_ref[...], 0.0)
    krp = krp_ref[...]
    ssq_kr = jnp.sum(krm * krm, axis=-1, keepdims=True)
    qc = qn_ref[...] * cos
    qs = qnp_ref[...] * sin
    kc = kn_ref[...] * cos
    ks = knp_ref[...] * sin
    hw = MLA_HEADS * LANE
    for h in range(MLA_HEADS):
        sl = slice(h * LANE, (h + 1) * LANE)
        xa = qx[:, sl]
        xb = qx[:, hw + h * LANE: hw + (h + 1) * LANE]
        r = _rms(xa, MLA_QK) * (MLA_QK ** -0.5 * LOG2E)
        q_out[0, h] = ((xa * qc + xb * qs) * r).astype(BF16)
        kn = kx[:, sl]
        rk = lax.rsqrt((jnp.sum(kn * kn, axis=-1, keepdims=True) + ssq_kr) * (1.0 / MLA_QK) + EPS)
        k_out[0, h] = (((kn + krm) * kc + krp * ks) * rk).astype(BF16)
        ones_lane = MLA_V if h % 2 == 0 else 0
        v_out[0, h] = jnp.where(lane == ones_lane, 1.0, vx[:, sl]).astype(BF16)


def _mla_prep(lat, cos_t, sin_t, qna, kvna, wq2, wk, wv, qn, qnp, kn, knp, bsz, seq):
    tm = _pick(seq, 512)
    nb = seq // tm
    hw = MLA_HEADS * LANE
    row = lambda b, s: b * nb + s
    const = lambda b, s: (0, 0)
    oshape = jax.ShapeDtypeStruct((bsz, MLA_HEADS, seq, LANE), BF16)
    ospec = pl.BlockSpec((1, MLA_HEADS, tm, LANE), lambda b, s: (b, 0, s, 0))
    return pl.pallas_call(
        _mla_prep_kernel,
        out_shape=(oshape, oshape, oshape),
        grid=(bsz, nb),
        in_specs=[pl.BlockSpec((tm, 2 * LANE), lambda b, s: (row(b, s), 0)),
                  pl.BlockSpec((tm, LANE), lambda b, s: (row(b, s), 2)),
                  pl.BlockSpec((tm, LANE), lambda b, s: (row(b, s), 3)),
                  pl.BlockSpec((tm, LANE), lambda b, s: (row(b, s), 4)),
                  pl.BlockSpec((tm, LANE), lambda b, s: (s, 0)),
                  pl.BlockSpec((tm, LANE), lambda b, s: (s, 0)),
                  pl.BlockSpec((1, MLA_Q_LORA), const),
                  pl.BlockSpec((1, MLA_KV_LORA), const),
                  pl.BlockSpec((MLA_Q_LORA, 2 * hw), const),
                  pl.BlockSpec((MLA_KV_LORA, hw), const),
                  pl.BlockSpec((MLA_KV_LORA, hw), const),
                  pl.BlockSpec((1, LANE), const), pl.BlockSpec((1, LANE), const),
                  pl.BlockSpec((1, LANE), const), pl.BlockSpec((1, LANE), const)],
        out_specs=(ospec, ospec, ospec),
        compiler_params=_cparams(("parallel", "arbitrary")),
        name="mla_prep",
    )(lat, lat, lat, lat, cos_t, sin_t, qna, kvna, wq2, wk, wv, qn, qnp, kn, knp)


def _attn_kernel(*refs, chunks):
    q_ref, kv, o_ref, s_scr = refs[0], refs[1:-2], refs[-2], refs[-1]
    tq = q_ref.shape[2]
    accs = []
    for hh in range(2):
        q = q_ref[0, hh]

        def scores(c, hh=hh, q=q):
            src, start, size = chunks[c]
            s = _dot_nt(q, kv[2 * src][0, hh, start:start + size, :])
            s_scr[hh, c % 2, :, :size] = s
            return jnp.max(s, axis=-1, keepdims=True)

        m = jnp.full((tq, 1), -jnp.inf, F32)
        acc = jnp.zeros((tq, LANE), F32)
        mt = scores(0)
        for c, (src, start, size) in enumerate(chunks):
            mt_next = scores(c + 1) if c + 1 < len(chunks) else None
            m_new = jnp.maximum(m, mt)
            p = jnp.exp2(s_scr[hh, c % 2, :, :size] - m_new)
            acc = jnp.exp2(m - m_new) * acc + jnp.dot(
                p.astype(BF16), kv[2 * src + 1][0, hh, start:start + size, :],
                preferred_element_type=F32)
            m, mt = m_new, mt_next
        accs.append(acc)
    lane = lax.broadcasted_iota(jnp.int32, (tq, LANE), 1)
    inv0 = 1.0 / accs[0][:, MLA_V:MLA_V + 1]
    inv1 = 1.0 / accs[1][:, 0:1]
    o_ref[0] = jnp.where(lane < MLA_V, accs[0] * inv0, accs[1] * inv1).astype(o_ref.dtype)


def _attention(q, kvs):
    bsz, nh, sq, _ = q.shape
    tq = _pick(sq, 512)
    chunks, in_specs, args = [], [pl.BlockSpec((1, 2, tq, LANE), lambda b, h, i: (b, h, i, 0))], [q]
    for src, (k, v) in enumerate(kvs):
        sk = k.shape[2]
        tk = next(t for t in (1024, 512, 256, 128, sk) if sk % t == 0)
        chunks += [(src, start, tk) for start in range(0, sk, tk)]
        in_specs += [pl.BlockSpec((1, 2, sk, LANE), lambda b, h, i: (b, h, 0, 0))] * 2
        args += [k, v]
    tk_max = max(size for _, _, size in chunks)
    return pl.pallas_call(
        functools.partial(_attn_kernel, chunks=tuple(chunks)),
        out_shape=jax.ShapeDtypeStruct((bsz, sq, nh * MLA_V), BF16),
        grid=(bsz, nh // 2, sq // tq),
        in_specs=in_specs,
        out_specs=pl.BlockSpec((1, tq, LANE), lambda b, h, i: (b, i, h)),
        scratch_shapes=[pltpu.VMEM((2, 2, tq, tk_max), F32)],
        compiler_params=_cparams(("parallel", "parallel", "arbitrary")),
        name="mla_attention",
    )(*args)


def _dot_nt(a, b):
    return lax.dot_general(a, b, (((1,), (1,)), ((), ())), preferred_element_type=F32)


def _dot_tn(a, b):
    return lax.dot_general(a, b, (((0,), (0,)), ((), ())), preferred_element_type=F32)


def _split_bf16(x, n):
    parts = []
    for _ in range(n):
        p = x.astype(BF16)
        parts.append(p)
        x = x - p.astype(F32)
    return parts


def _dot_exact_lhs(a, x):
    hi, lo = _split_bf16(x, 2)
    return (jnp.dot(a, lo, preferred_element_type=F32)
            + jnp.dot(a, hi, preferred_element_type=F32))


_SCAN_COMBOS = (('gla', False), ('gla', True), ('ret', False), ('ret', True))


def _scans_kernel(*refs, nchunk):
    n = len(_SCAN_COMBOS)
    ins, outs, scr = refs[:7 * n], refs[7 * n:9 * n], refs[9 * n:10 * n]
    tab_it = iter(refs[10 * n:])
    tabs = [next(tab_it) if kind == 'ret' else None for kind, _ in _SCAN_COMBOS]
    step = pl.program_id(1)
    blk = nchunk * CHUNK

    @pl.when(step == 0)
    def _():
        for i, (kind, reverse) in enumerate(_SCAN_COMBOS):
            scr[i][...] = ins[7 * i + 6][0]
            if kind == 'ret':
                _ret_decay_tables(ins[7 * i + 5], tabs[i], reverse, blk)

    chains = [_scan_chain(*ins[7 * i:7 * i + 6], outs[2 * i], scr[i], tabs[i],
                          kind=kind, reverse=reverse, nchunk=nchunk)
              for i, (kind, reverse) in enumerate(_SCAN_COMBOS)]
    while chains:
        for chain in list(chains):
            if next(chain, 'done') == 'done':
                chains.remove(chain)

    @pl.when(step == pl.num_programs(1) - 1)
    def _():
        for i in range(n):
            outs[2 * i + 1][0] = scr[i][...]


def _chunk_masks(blk, reverse):
    ii = lax.broadcasted_iota(jnp.int32, (blk, blk), 0)
    jj = lax.broadcasted_iota(jnp.int32, (blk, blk), 1)
    same = (ii // CHUNK) == (jj // CHUNK)
    return same, (same & (jj > ii)) if reverse else (same & (ii >= jj))


def _ret_decay_tables(rd_ref, tab_ref, reverse, blk):
    ii = lax.broadcasted_iota(jnp.int32, (blk, blk), 0)
    jj = lax.broadcasted_iota(jnp.int32, (blk, blk), 1)
    _, amask = _chunk_masks(blk, reverse)
    dd = jnp.where(amask, (jj - ii if reverse else ii - jj).astype(F32), 0.0)
    pidx = (lax.broadcasted_iota(jnp.int32, (blk, LANE), 0) % CHUNK).astype(F32)
    for h in range(4):
        lg = -jnp.exp(rd_ref[h:h + 1, :])
        lgb = jnp.concatenate([lg] * (blk // LANE), axis=1) if blk > LANE else lg[:, :blk]
        tab_ref[h, :, :blk] = jnp.where(amask, jnp.exp(dd * lgb), 0.0)
        if reverse:
            zeta, xi = jnp.exp(pidx * lg), jnp.exp((CHUNK - pidx) * lg)
        else:
            zeta, xi = jnp.exp((CHUNK - 1 - pidx) * lg), jnp.exp((pidx + 1.0) * lg)
        tab_ref[h, :, blk:blk + LANE] = zeta
        tab_ref[h, :, blk + LANE:] = xi


def _scan_chain(q_ref, k_ref, v_ref, x0_ref, x1_ref, x2_ref, o_ref, st_scr, tab_ref, *,
                kind, reverse, nchunk):
    if kind == 'gla':
        r_ref, w2_ref, bg_ref = x0_ref, x1_ref, x2_ref
    else:
        cos_ref, sin_ref, rd_ref = x0_ref, x1_ref, x2_ref

    blk = nchunk * CHUNK
    ii = lax.broadcasted_iota(jnp.int32, (blk, blk), 0)
    jj = lax.broadcasted_iota(jnp.int32, (blk, blk), 1)
    same, amask = _chunk_masks(blk, reverse)
    order = range(nchunk - 1, -1, -1) if reverse else range(nchunk)
    vb = v_ref[...].astype(BF16)

    if kind == 'gla':
        pre = jnp.dot(r_ref[...].astype(BF16), w2_ref[...],
                      preferred_element_type=F32) + bg_ref[...]
        yield
        la = jax.nn.log_sigmoid(pre) * (1.0 / GLA_GATE_NORMALIZER)
        tri = (same & ((jj >= ii) if reverse else (ii >= jj))).astype(BF16)
        cum = _dot_exact_lhs(tri, la)
        yield
        totb = jnp.concatenate(
            [jnp.broadcast_to(cum[c * CHUNK + (0 if reverse else CHUNK - 1)][None, :],
                              (CHUNK, cum.shape[1])) for c in range(nchunk)], axis=0)
        k32 = k_ref[...].astype(F32)
        qd = (q_ref[...].astype(F32) * ((GLA_DK ** -0.5) * jnp.exp(cum))).astype(BF16)
        kd = (k32 * jnp.exp(-cum)).astype(BF16)
        kz = (k32 * jnp.exp(totb - cum)).astype(BF16)
        dtot = jnp.exp(totb)
    else:
        cs = cos_ref[...]
        sn = sin_ref[...]

    for h in range(4):
        cols = slice(h * LANE, (h + 1) * LANE)
        vh = vb[:, cols]
        if kind == 'gla':
            qh, kzh = qd[:, cols], kz[:, cols]
            att = jnp.where(amask, _dot_nt(qh, kd[:, cols]), 0.0)
            xi = None
        else:
            lg = -jnp.exp(rd_ref[h:h + 1, :])
            q = q_ref[:, cols].astype(F32)
            k = k_ref[:, cols].astype(F32)
            qh = (q * cs + pltpu.roll(q, LANE // 2, 1) * sn).astype(BF16)
            kr = (k * cs + pltpu.roll(k, LANE // 2, 1) * sn) * (RET_DK ** -0.5)
            xi = tab_ref[h, :, blk + LANE:]
            kzh = (kr * tab_ref[h, :, blk:blk + LANE]).astype(BF16)
            att = _dot_nt(qh, kr.astype(BF16)) * tab_ref[h, :, :blk]
            gch = jnp.exp(CHUNK * lg)
        yield
        o_intra = jnp.dot(att.astype(BF16), vh, preferred_element_type=F32)
        upd = [_dot_tn(vh[c * CHUNK:(c + 1) * CHUNK], kzh[c * CHUNK:(c + 1) * CHUNK])
               for c in range(nchunk)]
        yield
        st = st_scr[h]
        for c in order:
            rows = slice(c * CHUNK, (c + 1) * CHUNK)
            o = o_intra[rows] + (_dot_nt(qh[rows], st.astype(BF16)) if xi is None
                                 else _dot_nt(qh[rows], st.astype(BF16)) * xi[rows])
            decay = dtot[c * CHUNK:c * CHUNK + 1, cols] if kind == 'gla' else gch
            st = st * decay + upd[c]
            o_ref[rows, cols] = o
        st_scr[h] = st
        yield


def _scans(mg, lat, bsz, seq, states, gla_extra, ret_extra):
    blk = _pick(seq, 256, CHUNK)
    nb = seq // blk
    const = lambda b, s: (0, 0)
    sspec = pl.BlockSpec((1, 4, LANE, LANE), lambda b, s: (b, 0, 0, 0))
    in_specs, args, out_specs = [], [], []
    for (kind, reverse), s0 in zip(_SCAN_COMBOS, states):
        pos = (lambda s: nb - 1 - s) if reverse else (lambda s: s)
        row = lambda b, s, pos=pos: b * nb + pos(s)
        col0 = 0 if kind == 'gla' else 4
        in_specs += [pl.BlockSpec((blk, 512), lambda b, s, c=col0 + j, row=row: (row(b, s), c))
                     for j in range(3)]
        args += [mg, mg, mg]
        if kind == 'gla':
            in_specs += [pl.BlockSpec((blk, LANE), lambda b, s, row=row: (row(b, s), 3)),
                         pl.BlockSpec((LANE, 512), const), pl.BlockSpec((1, 512), const)]
            args += [lat, *gla_extra[reverse]]
        else:
            in_specs += [pl.BlockSpec((blk, LANE), lambda b, s, pos=pos: (pos(s), 0)),
                         pl.BlockSpec((blk, LANE), lambda b, s, pos=pos: (pos(s), 0)),
                         pl.BlockSpec((SUBLANE, LANE), const)]
            args += [ret_extra[0], ret_extra[1], ret_extra[2][reverse]]
        in_specs.append(sspec)
        args.append(s0)
        out_specs += [pl.BlockSpec((blk, 512), lambda b, s, row=row: (row(b, s), 0)), sspec]
    r = bsz * seq
    n = len(_SCAN_COMBOS)
    outs = pl.pallas_call(
        functools.partial(_scans_kernel, nchunk=blk // CHUNK),
        out_shape=(jax.ShapeDtypeStruct((r, 512), F32),
                   jax.ShapeDtypeStruct((bsz, 4, LANE, LANE), F32)) * n,
        grid=(bsz, nb),
        in_specs=in_specs,
        out_specs=out_specs,
        scratch_shapes=([pltpu.VMEM((4, LANE, LANE), F32)] * n
                        + [pltpu.VMEM((4, blk, blk + 2 * LANE), F32)
                           for kind, _ in _SCAN_COMBOS if kind == 'ret']),
        compiler_params=_cparams(("parallel", "arbitrary")),
        name="recurrent_scans",
    )(*args)
    return list(outs[0::2]), list(outs[1::2])


def _merge_kernel(ym_ref, ogf_ref, ogb_ref, orf_ref, orb_ref, gg_ref, gr_ref, g0_ref, g1_ref,
                  g2_ref, on_ref, bg_ref, wb_ref, wo_ref, h_ref, gt_ref, o_ref):
    def gated_head_norm(of_ref, ob_ref, g_ref, w, rows):
        parts = []
        for hd in range(4):
            cols = slice(hd * LANE, (hd + 1) * LANE)
            osum = of_ref[rows, cols] + ob_ref[rows, cols]
            y = osum * _rms(osum, LANE)
            if w is not None:
                y = y * w
            g = g_ref[rows, cols].astype(F32)
            parts.append((y * (g * jax.nn.sigmoid(g))).astype(BF16))
        return jnp.concatenate(parts, axis=1)

    def rows_chain(rows):
        ys = (ym_ref[rows, :], gated_head_norm(ogf_ref, ogb_ref, gg_ref, on_ref[...], rows),
              gated_head_norm(orf_ref, orb_ref, gr_ref, None, rows))
        ts = [jnp.dot(y, wb_ref[n], preferred_element_type=F32) for n, y in enumerate(ys)]
        yield
        u = None
        for n, g_ref in enumerate((g0_ref, g1_ref, g2_ref)):
            cols = slice(n * D_MODEL, (n + 1) * D_MODEL)
            t = jax.nn.sigmoid(g_ref[rows, :].astype(F32) + bg_ref[:, cols]) * ts[n]
            u = t if u is None else u + t
        out = jnp.dot(u.astype(BF16), wo_ref[...], preferred_element_type=F32)
        o_ref[rows, :] = h_ref[rows, :] + gt_ref[0] * out
        yield

    tm = h_ref.shape[0]
    sub = tm // 2 if tm % (2 * HALO) == 0 else tm
    chains = [rows_chain(slice(r0, r0 + sub)) for r0 in range(0, tm, sub)]
    while chains:
        for chain in list(chains):
            if next(chain, 'done') == 'done':
                chains.remove(chain)


def _merge(y_mla, scan_o, mg, on, bg, wb, wo, h, gt):
    r, d = h.shape
    g = gt.shape[0]
    rpg = r // g
    tm = _pick(rpg, 512)
    yspec = pl.BlockSpec((tm, BRANCH_W), lambda i: (i, 0))
    gspec = lambda c: pl.BlockSpec((tm, BRANCH_W), lambda i, c=c: (i, c))
    mspec = lambda c: pl.BlockSpec((tm, d), lambda i, c=c: (i, c))
    return pl.pallas_call(
        _merge_kernel,
        out_shape=jax.ShapeDtypeStruct((r, d), F32),
        grid=(r // tm,),
        in_specs=[yspec] * 5 + [gspec(3), gspec(7), mspec(4), mspec(5), mspec(6),
                  pl.BlockSpec((1, LANE), lambda i: (0, 0)),
                  pl.BlockSpec((1, GATE_W), lambda i: (0, 0)),
                  pl.BlockSpec((N_BRANCH, BRANCH_W, d), lambda i: (0, 0, 0)),
                  pl.BlockSpec((d, d), lambda i: (0, 0)),
                  pl.BlockSpec((tm, d), lambda i: (i, 0)),
                  pl.BlockSpec((1, 1, d), lambda i: (i * tm // rpg, 0, 0))],
        out_specs=pl.BlockSpec((tm, d), lambda i: (i, 0)),
        compiler_params=_cparams(("parallel",)),
        name="gated_merge",
    )(y_mla, *scan_o, mg, mg, mg, mg, mg, on, bg, wb, wo, h, gt)


def _ffn_kernel(hp_ref, h_ref, hn_ref, sh_ref, sc_ref, gt_ref, nw_ref, wg_ref, wu_ref, wdw_ref,
                bdw_ref, wo_ref, o_ref, act_scr, *, tm, seq, tf):
    i = pl.program_id(0)

    def norm_mod(x):
        y = x * lax.rsqrt(jnp.mean(x * x, axis=-1, keepdims=True) + EPS) * nw_ref[...]
        return y * (1.0 + sc_ref[0]) + sh_ref[0]

    row0 = i * tm
    prev_ok = (row0 % seq != 0).astype(F32)
    next_ok = ((row0 + tm) % seq != 0).astype(F32)

    a_mid = norm_mod(h_ref[...]).astype(BF16)
    a_ext = jnp.concatenate([(norm_mod(hp_ref[...]) * prev_ok).astype(BF16), a_mid,
                             (norm_mod(hn_ref[...]) * next_ok).astype(BF16)], axis=0)
    nchunk = wg_ref.shape[1] // tf

    def hidden(j):
        cs = slice(j * tf, (j + 1) * tf)
        gate = jnp.dot(a_ext, wg_ref[:, cs], preferred_element_type=F32)
        up = jnp.dot(a_mid, wu_ref[:, cs], preferred_element_type=F32)
        return gate, up

    for j in range(nchunk):
        cs = slice(j * tf, (j + 1) * tf)
        gate, up = hidden(j)
        g_prev = pltpu.roll(gate, 1, 0)[HALO:HALO + tm]
        g_next = pltpu.roll(gate, tm + 2 * HALO - 1, 0)[HALO:HALO + tm]
        conv = (g_prev * wdw_ref[0:1, cs] + gate[HALO:HALO + tm] * wdw_ref[1:2, cs]
                + g_next * wdw_ref[2:3, cs] + bdw_ref[:, cs])
        act_scr[:, cs] = (jax.nn.gelu(conv, approximate=True) * up).astype(BF16)
    out = jnp.dot(act_scr[...], wo_ref[...], preferred_element_type=F32)
    o_ref[...] = h_ref[...] + gt_ref[0] * out


def _ffn(h, shift, scale, gt, nw, wg, wu, wdw, bdw, wo, seq):
    r, d = h.shape
    g = shift.shape[0]
    rpg = r // g
    dff = wg.shape[1]
    tm = _pick(seq, 512, HALO)
    tf = _pick(dff, 2 * LANE, LANE)
    nsub = tm // HALO
    nrb = r // HALO
    gidx = lambda i: (i * tm // rpg, 0, 0)
    const = lambda i: (0, 0)
    kern = functools.partial(_ffn_kernel, tm=tm, seq=seq, tf=tf)
    return pl.pallas_call(
        kern,
        out_shape=jax.ShapeDtypeStruct((r, d), F32),
        grid=(r // tm,),
        in_specs=[pl.BlockSpec((HALO, d), lambda i: (jnp.maximum(i * nsub - 1, 0), 0)),
                  pl.BlockSpec((tm, d), lambda i: (i, 0)),
                  pl.BlockSpec((HALO, d), lambda i: (jnp.minimum((i + 1) * nsub, nrb - 1), 0)),
                  pl.BlockSpec((1, 1, d), gidx), pl.BlockSpec((1, 1, d), gidx),
                  pl.BlockSpec((1, 1, d), gidx),
                  pl.BlockSpec((1, d), const),
                  pl.BlockSpec((d, dff), const),
                  pl.BlockSpec((d, dff), const),
                  pl.BlockSpec((3, dff), const),
                  pl.BlockSpec((1, dff), const),
                  pl.BlockSpec((dff, d), const)],
        out_specs=pl.BlockSpec((tm, d), lambda i: (i, 0)),
        scratch_shapes=[pltpu.VMEM((tm, dff), BF16)],
        compiler_params=_cparams(("parallel",)),
        name="conv_ffn",
    )(h, h, h, shift, scale, gt, nw, wg, wu, wdw, bdw, wo)


def _mla_weight_layout():
    pidx, psgn = _rope_partner(MLA_ROPE)
    qa_idx, qa_sgn, qb_idx, qb_sgn = [], [], [], []
    k_idx, k_sgn, v_idx, v_sgn = [], [], [], []
    for h in range(MLA_HEADS):
        base = h * MLA_QK
        qa_idx += list(range(base, base + MLA_QK)) + [0] * 32
        qa_sgn += [1.0] * MLA_QK + [0.0] * 32
        qb_idx += [0] * 64 + (base + MLA_NOPE + pidx).tolist() + [0] * 32
        qb_sgn += [0.0] * 64 + psgn.tolist() + [0.0] * 32
        kb = h * (MLA_NOPE + MLA_V)
        k_idx += list(range(kb, kb + MLA_NOPE)) + [0] * 64
        k_sgn += [1.0] * 64 + [0.0] * 64
        vcols = list(range(kb + MLA_NOPE, kb + MLA_NOPE + MLA_V))
        if h % 2 == 0:
            v_idx += vcols + [0] * 64
            v_sgn += [1.0] * 64 + [0.0] * 64
        else:
            v_idx += [0] * 64 + vcols
            v_sgn += [0.0] * 64 + [1.0] * 64
    f = lambda a, t: np.asarray(a, t)
    return (f(qa_idx + qb_idx, np.int32), f(qa_sgn + qb_sgn, np.float32),
            f(k_idx, np.int32), f(k_sgn, np.float32), f(v_idx, np.int32), f(v_sgn, np.float32))


def _slot_vec(w):
    pidx, _ = _rope_partner(MLA_ROPE)
    z32 = jnp.zeros((32,), F32)
    a = jnp.concatenate([w, z32])
    p = jnp.concatenate([jnp.zeros((64,), F32), w[MLA_NOPE + pidx], z32])
    return a.reshape(1, LANE), p.reshape(1, LANE)


def _mla_tables(seq, rope):
    ones = jnp.ones((seq, 64), F32)
    if not rope:
        return jnp.ones((seq, LANE), F32), jnp.zeros((seq, LANE), F32)
    pos = jnp.arange(seq)
    dim = MLA_ROPE // 2
    inv = ROPE_THETA ** (-jnp.arange(dim // 2, dtype=F32) * 2.0 / dim)
    ar = (pos // GRID_W).astype(F32)[:, None] * inv[None, :]
    ac = (pos % GRID_W).astype(F32)[:, None] * inv[None, :]
    cos = jnp.concatenate([ones, jnp.cos(ar), jnp.cos(ar), jnp.cos(ac), jnp.cos(ac),
                           jnp.ones((seq, 32), F32)], axis=1)
    sin = jnp.concatenate([0.0 * ones, jnp.sin(ar), jnp.sin(ar), jnp.sin(ac), jnp.sin(ac),
                           jnp.zeros((seq, 32), F32)], axis=1)
    return cos, sin


def _ret_tables(start, seq):
    inv = 1.0 / (RET_THETA ** jnp.linspace(0.0, 1.0, RET_DK // 2, dtype=F32))
    ang = (start + jnp.arange(seq)).astype(F32)[:, None] * inv[None, :]
    cos, sin = jnp.cos(ang), jnp.sin(ang)
    return jnp.concatenate([cos, cos], axis=1), jnp.concatenate([-sin, sin], axis=1)


def _split_in_proj(w):
    o = _IN_OFF
    z = lambda n: jnp.zeros((w.shape[0], n), w.dtype)
    kr = o['mla_kr']
    q8 = MLA_ROPE // 4
    partner = []
    for base in (kr, kr + 2 * q8):
        partner += [-w[:, base + q8:base + 2 * q8], w[:, base:base + q8]]
    w_lat = jnp.concatenate(
        [w[:, :kr], z(64), w[:, kr:kr + MLA_ROPE], w[:, o['gla_rf']:o['gla_rf'] + 2 * GLA_GATE_RANK],
         z(64)] + partner + [z(32)], axis=1)
    w_mg = jnp.concatenate([w[:, o['gla_q']:o['gla_rf']], w[:, o['ret_q']:]], axis=1)
    return w_lat.astype(BF16), w_mg.astype(BF16)


def _mixers(lat, mg, bsz, seq, lw, mla_tabs, ret_tabs, states):
    qkv = _mla_prep(lat, mla_tabs[0], mla_tabs[1], lw['qna'], lw['kvna'], lw['wq2'], lw['wk'],
                    lw['wv'], lw['qn'], lw['qnp'], lw['kn'], lw['knp'], bsz, seq)
    scan_o, finals = _scans(mg, lat, bsz, seq, states,
                            {False: (lw['w2f'], lw['bgf']), True: (lw['w2b'], lw['bgb'])},
                            (ret_tabs[0], ret_tabs[1], {False: lw['rd_f'], True: lw['rd_b']}))
    return qkv, scan_o, finals


def kernel(x, c, ctx, c_ctx, w_ada, b_ada, norm1_w, norm2_w, w_in, b_gate, mla_q_norm_a, mla_w_qb,
           mla_kv_norm_a, mla_w_kvb, mla_q_norm, mla_k_norm, gla_w_gk2, gla_b_gk, gla_o_norm,
           ret_decay, w_branch, w_out, w_ffn_in, w_dw, b_dw, w_ffn_out):
    bsz, seq, d = x.shape
    clen = ctx.shape[1]
    depth = w_ada.shape[0]
    r_lat, r_ctx = bsz * seq, bsz * clen

    npad = -(bsz + 1) % SUBLANE
    cc = jnp.concatenate([c, c_ctx[None, :], jnp.zeros((npad, d), F32)], axis=0)
    mod_all = _ada_mod(cc, w_ada, b_ada)

    qidx, qsgn, kidx, ksgn, vidx, vsgn = _mla_weight_layout()
    mla_lat_tabs = _mla_tables(seq, True)
    mla_ctx_tabs = _mla_tables(clen, False)
    ret_lat_tabs = _ret_tables(clen, seq)
    ret_ctx_tabs = _ret_tables(0, clen)
    zstate = jnp.zeros((bsz, 4, LANE, LANE), F32)

    h = x.reshape(r_lat, d)
    hc = ctx.reshape(r_ctx, d)
    for l in range(depth):
        need_ctx = l < depth - 1
        mods = mod_all[l].reshape(-1, 6, d)
        ml = [mods[:bsz, j].reshape(bsz, 1, d) for j in range(6)]
        mc = [mods[bsz:bsz + 1, j].reshape(1, 1, d) for j in range(6)]

        w_lat, w_mg = _split_in_proj(w_in[l])
        qn, qnp = _slot_vec(mla_q_norm[l])
        kn, knp = _slot_vec(mla_k_norm[l])
        w2 = gla_w_gk2[l]
        zr = jnp.zeros((LANE - 2 * GLA_GATE_RANK, 4 * GLA_DK), F32)
        z16 = jnp.zeros((GLA_GATE_RANK, 4 * GLA_DK), F32)
        lw = dict(
            qna=mla_q_norm_a[l].reshape(1, -1), kvna=mla_kv_norm_a[l].reshape(1, -1),
            wq2=(mla_w_qb[l][:, qidx] * qsgn[None, :]).astype(BF16),
            wk=(mla_w_kvb[l][:, kidx] * ksgn[None, :]).astype(BF16),
            wv=(mla_w_kvb[l][:, vidx] * vsgn[None, :]).astype(BF16),
            qn=qn, qnp=qnp, kn=kn, knp=knp,
            w2f=jnp.concatenate([zr, w2[0], z16], axis=0).astype(BF16),
            w2b=jnp.concatenate([zr, z16, w2[1]], axis=0).astype(BF16),
            bgf=gla_b_gk[l, 0].reshape(1, -1), bgb=gla_b_gk[l, 1].reshape(1, -1),
            gla_on=gla_o_norm[l].reshape(1, LANE),
            rd_f=jnp.broadcast_to(jnp.pad(ret_decay[l, 0], (0, 4))[:, None], (SUBLANE, LANE)),
            rd_b=jnp.broadcast_to(jnp.pad(ret_decay[l, 1], (0, 4))[:, None], (SUBLANE, LANE)),
        )
        nw1 = norm1_w[l].reshape(1, d)
        nw2 = norm2_w[l].reshape(1, d)
        bg = b_gate[l].reshape(1, GATE_W)
        wb = w_branch[l].astype(BF16)
        wo = w_out[l].astype(BF16)
        wg = w_ffn_in[l][:, :D_FF].astype(BF16)
        wu = w_ffn_in[l][:, D_FF:].astype(BF16)
        wfo = w_ffn_out[l].astype(BF16)
        bdw = b_dw[l].reshape(1, D_FF)

        def project(hh, m, name):
            return _in_proj(hh, m[0], m[1], nw1, w_lat, w_mg, 2048, 1024, name)

        lat_c, mg_c = project(hc, mc, "in_proj_ctx")
        (q_c, k_c, v_c), so_c, states = _mixers(
            lat_c, mg_c, bsz, clen, lw, mla_ctx_tabs, ret_ctx_tabs, (zstate,) * 4)
        lat_l, mg_l = project(h, ml, "in_proj")
        (q_l, k_l, v_l), so_l, _ = _mixers(
            lat_l, mg_l, bsz, seq, lw, mla_lat_tabs, ret_lat_tabs, states)
        y_mla = _attention(q_l, ((k_c, v_c), (k_l, v_l))).reshape(r_lat, BRANCH_W)
        h = _merge(y_mla, so_l, mg_l, lw['gla_on'], bg, wb, wo, h, ml[2])
        h = _ffn(h, ml[3], ml[4], ml[5], nw2, wg, wu, w_dw[l], bdw, wfo, seq)
        if need_ctx:
            ym_c = _attention(q_c, ((k_c, v_c),)).reshape(r_ctx, BRANCH_W)
            hc = _merge(ym_c, so_c, mg_c, lw['gla_on'], bg, wb, wo, hc, mc[2])
            hc = _ffn(hc, mc[3], mc[4], mc[5], nw2, wg, wu, w_dw[l], bdw, wfo, clen)
    return h.reshape(bsz, seq, d)
```

```python
import functools
import math

import numpy as np
import jax
import jax.numpy as jnp
from jax import lax
from jax.experimental import pallas as pl
from jax.experimental.pallas import tpu as pltpu

F32 = jnp.float32
BF16 = jnp.bfloat16

D_MODEL = 1024
GRID_W = 64
N_BRANCH = 3
BRANCH_W = 512
MLA_HEADS = 8
MLA_NOPE = 64
MLA_ROPE = 32
MLA_QK = MLA_NOPE + MLA_ROPE
MLA_V = BRANCH_W // MLA_HEADS
MLA_Q_LORA = 256
MLA_KV_LORA = 128
GLA_HEADS = 4
GLA_DK = 128
GLA_DV = 128
GLA_GATE_RANK = 16
GLA_GATE_NORMALIZER = 16.0
RET_HEADS = 4
RET_DK = 128
RET_DV = 128
D_FF = 2816
CHUNK = 64
ROPE_THETA = 10000.0
RET_THETA = 10000.0
EPS = 1e-6
LOG2E = math.log2(math.e)
LANE = 128
SUBLANE = 8
HALO = 16
VMEM_LIMIT = 56 * 1024 * 1024

_IN_WIDTHS = (('mla_q', 256), ('mla_kv', 128), ('mla_kr', 32), ('gla_q', 512), ('gla_k', 512),
              ('gla_v', 512), ('gla_g', 512), ('gla_rf', 16), ('gla_rb', 16), ('ret_q', 512),
              ('ret_k', 512), ('ret_v', 512), ('ret_g', 512), ('gate_mla', 1024),
              ('gate_gla', 1024), ('gate_ret', 1024))
_IN_OFF = {}
_o = 0
for _n, _w in _IN_WIDTHS:
    _IN_OFF[_n] = _o
    _o += _w
N_IN = _o
LAT_W = 5 * LANE
MIX_W = 8 * 512
GATE_W = 3 * D_MODEL


def _cparams(sem):
    return pltpu.CompilerParams(dimension_semantics=sem, vmem_limit_bytes=VMEM_LIMIT)


def _pick(n, pref, mult=SUBLANE):
    if n <= pref:
        return n
    for t in range(pref - pref % mult, 0, -mult):
        if n % t == 0:
            return t
    return n


def _rope_partner(nrot):
    half = nrot // 2
    q = half // 2
    idx = np.zeros(nrot, np.int32)
    sgn = np.zeros(nrot, np.float32)
    for base in (0, half):
        for i in range(q):
            idx[base + i] = base + i + q
            sgn[base + i] = -1.0
            idx[base + q + i] = base + i
            sgn[base + q + i] = 1.0
    return idx, sgn


def _ada_kernel(c_ref, w_ref, b_ref, o_ref):
    c = c_ref[...]
    cs = c * jax.nn.sigmoid(c)
    o_ref[0] = jnp.dot(cs, w_ref[0], preferred_element_type=F32,
                       precision=lax.Precision.HIGHEST) + b_ref[0]


def _ada_mod(cc, w_ada, b_ada):
    nl, d, n = w_ada.shape
    tn = _pick(n, 1536, LANE)
    return pl.pallas_call(
        _ada_kernel,
        out_shape=jax.ShapeDtypeStruct((nl, cc.shape[0], n), F32),
        grid=(nl, n // tn),
        in_specs=[pl.BlockSpec((cc.shape[0], d), lambda l, j: (0, 0)),
                  pl.BlockSpec((1, d, tn), lambda l, j: (l, 0, j)),
                  pl.BlockSpec((1, 1, tn), lambda l, j: (l, 0, j))],
        out_specs=pl.BlockSpec((1, cc.shape[0], tn), lambda l, j: (l, 0, j)),
        compiler_params=_cparams(("arbitrary", "arbitrary")),
        name="ada_mod",
    )(cc, w_ada, b_ada.reshape(nl, 1, n))


def _in_proj_kernel(x_ref, sh_ref, sc_ref, nw_ref, wl_ref, w_ref, lat_ref, o_ref, a_scr):
    @pl.when(pl.program_id(1) == 0)
    def _():
        x = x_ref[...]
        y = x * lax.rsqrt(jnp.mean(x * x, axis=-1, keepdims=True) + EPS) * nw_ref[...]
        a = (y * (1.0 + sc_ref[0]) + sh_ref[0]).astype(BF16)
        a_scr[...] = a
        lat_ref[...] = jnp.dot(a, wl_ref[...], preferred_element_type=F32)

    o_ref[...] = jnp.dot(a_scr[...], w_ref[...], preferred_element_type=F32).astype(o_ref.dtype)


def _in_proj(x, shift, scale, nw, w_lat, w_mg, tm, tn, name):
    r, d = x.shape
    g = shift.shape[0]
    rpg = r // g
    n = w_mg.shape[1]
    nl = w_lat.shape[1]
    tm = _pick(rpg, tm)
    tn = _pick(n, tn, LANE)
    return pl.pallas_call(
        _in_proj_kernel,
        out_shape=(jax.ShapeDtypeStruct((r, nl), F32), jax.ShapeDtypeStruct((r, n), BF16)),
        grid=(r // tm, n // tn),
        in_specs=[pl.BlockSpec((tm, d), lambda i, j: (i, 0)),
                  pl.BlockSpec((1, 1, d), lambda i, j: (i * tm // rpg, 0, 0)),
                  pl.BlockSpec((1, 1, d), lambda i, j: (i * tm // rpg, 0, 0)),
                  pl.BlockSpec((1, d), lambda i, j: (0, 0)),
                  pl.BlockSpec((d, nl), lambda i, j: (0, 0)),
                  pl.BlockSpec((d, tn), lambda i, j: (0, j))],
        out_specs=(pl.BlockSpec((tm, nl), lambda i, j: (i, 0)),
                   pl.BlockSpec((tm, tn), lambda i, j: (i, j))),
        scratch_shapes=[pltpu.VMEM((tm, d), BF16)],
        compiler_params=_cparams(("parallel", "arbitrary")),
        name=name,
    )(x, shift, scale, nw, w_lat, w_mg)


def _rms(x, n):
    return lax.rsqrt(jnp.sum(x * x, axis=-1, keepdims=True) * (1.0 / n) + EPS)


def _mla_prep_kernel(cq_ref, ckv_ref, kr_ref, krp_ref, cos_ref, sin_ref, qna_ref, kvna_ref,
                     wq_ref, wk_ref, wv_ref, qn_ref, qnp_ref, kn_ref, knp_ref,
                     q_out, k_out, v_out):
    cos = cos_ref[...]
    sin = sin_ref[...]
    cq = cq_ref[...]
    cqn = (cq * _rms(cq, MLA_Q_LORA) * qna_ref[...]).astype(BF16)
    qx = jnp.dot(cqn, wq_ref[...], preferred_element_type=F32)
    ckv = ckv_ref[...]
    ckvn = (ckv * _rms(ckv, MLA_KV_LORA) * kvna_ref[...]).astype(BF16)
    kx = jnp.dot(ckvn, wk_ref[...], preferred_element_type=F32)
    vx = jnp.dot(ckvn, wv_ref[...], preferred_element_type=F32)
    lane = lax.broadcasted_iota(jnp.int32, kr_ref.shape, 1)
    krm = jnp.where(lane < MLA_QK, kr_ref[...], 0.0)
    krp = krp_ref[...]
    ssq_kr = jnp.sum(krm * krm, axis=-1, keepdims=True)
    qc = qn_ref[...] * cos
    qs = qnp_ref[...] * sin
    kc = kn_ref[...] * cos
    ks = knp_ref[...] * sin
    hw = MLA_HEADS * LANE
    for h in range(MLA_HEADS):
        sl = slice(h * LANE, (h + 1) * LANE)
        xa = qx[:, sl]
        xb = qx[:, hw + h * LANE: hw + (h + 1) * LANE]
        r = _rms(xa, MLA_QK) * (MLA_QK ** -0.5 * LOG2E)
        q_out[0, h] = ((xa * qc + xb * qs) * r).astype(BF16)
        kn = kx[:, sl]
        rk = lax.rsqrt((jnp.sum(kn * kn, axis=-1, keepdims=True) + ssq_kr) * (1.0 / MLA_QK) + EPS)
        k_out[0, h] = (((kn + krm) * kc + krp * ks) * rk).astype(BF16)
        ones_lane = MLA_V if h % 2 == 0 else 0
        v_out[0, h] = jnp.where(lane == ones_lane, 1.0, vx[:, sl]).astype(BF16)


def _mla_prep(lat, cos_t, sin_t, qna, kvna, wq2, wk, wv, qn, qnp, kn, knp, bsz, seq):
    tm = _pick(seq, 512)
    nb = seq // tm
    hw = MLA_HEADS * LANE
    row = lambda b, s: b * nb + s
    const = lambda b, s: (0, 0)
    oshape = jax.ShapeDtypeStruct((bsz, MLA_HEADS, seq, LANE), BF16)
    ospec = pl.BlockSpec((1, MLA_HEADS, tm, LANE), lambda b, s: (b, 0, s, 0))
    return pl.pallas_call(
        _mla_prep_kernel,
        out_shape=(oshape, oshape, oshape),
        grid=(bsz, nb),
        in_specs=[pl.BlockSpec((tm, 2 * LANE), lambda b, s: (row(b, s), 0)),
                  pl.BlockSpec((tm, LANE), lambda b, s: (row(b, s), 2)),
                  pl.BlockSpec((tm, LANE), lambda b, s: (row(b, s), 3)),
                  pl.BlockSpec((tm, LANE), lambda b, s: (row(b, s), 4)),
                  pl.BlockSpec((tm, LANE), lambda b, s: (s, 0)),
                  pl.BlockSpec((tm, LANE), lambda b, s: (s, 0)),
                  pl.BlockSpec((1, MLA_Q_LORA), const),
                  pl.BlockSpec((1, MLA_KV_LORA), const),
                  pl.BlockSpec((MLA_Q_LORA, 2 * hw), const),
                  pl.BlockSpec((MLA_KV_LORA, hw), const),
                  pl.BlockSpec((MLA_KV_LORA, hw), const),
                  pl.BlockSpec((1, LANE), const), pl.BlockSpec((1, LANE), const),
                  pl.BlockSpec((1, LANE), const), pl.BlockSpec((1, LANE), const)],
        out_specs=(ospec, ospec, ospec),
        compiler_params=_cparams(("parallel", "arbitrary")),
        name="mla_prep",
    )(lat, lat, lat, lat, cos_t, sin_t, qna, kvna, wq2, wk, wv, qn, qnp, kn, knp)


def _attn_kernel(*refs, chunks):
    q_ref, kv, o_ref, s_scr = refs[0], refs[1:-2], refs[-2], refs[-1]
    tq = q_ref.shape[2]
    accs = []
    for hh in range(2):
        q = q_ref[0, hh]

        def scores(c, hh=hh, q=q):
            src, start, size = chunks[c]
            s = _dot_nt(q, kv[2 * src][0, hh, start:start + size, :])
            s_scr[hh, c % 2, :, :size] = s
            return jnp.max(s, axis=-1, keepdims=True)

        m = jnp.full((tq, 1), -jnp.inf, F32)
        acc = jnp.zeros((tq, LANE), F32)
        mt = scores(0)
        for c, (src, start, size) in enumerate(chunks):
            mt_next = scores(c + 1) if c + 1 < len(chunks) else None
            m_new = jnp.maximum(m, mt)
            p = jnp.exp2(s_scr[hh, c % 2, :, :size] - m_new)
            acc = jnp.exp2(m - m_new) * acc + jnp.dot(
                p.astype(BF16), kv[2 * src + 1][0, hh, start:start + size, :],
                preferred_element_type=F32)
            m, mt = m_new, mt_next
        accs.append(acc)
    lane = lax.broadcasted_iota(jnp.int32, (tq, LANE), 1)
    inv0 = 1.0 / accs[0][:, MLA_V:MLA_V + 1]
    inv1 = 1.0 / accs[1][:, 0:1]
    o_ref[0] = jnp.where(lane < MLA_V, accs[0] * inv0, accs[1] * inv1).astype(o_ref.dtype)


def _attention(q, kvs):
    bsz, nh, sq, _ = q.shape
    tq = _pick(sq, 512)
    chunks, in_specs, args = [], [pl.BlockSpec((1, 2, tq, LANE), lambda b, h, i: (b, h, i, 0))], [q]
    for src, (k, v) in enumerate(kvs):
        sk = k.shape[2]
        tk = next(t for t in (2048, 1024, 512, 256, 128, sk) if sk % t == 0)
        chunks += [(src, start, tk) for start in range(0, sk, tk)]
        in_specs += [pl.BlockSpec((1, 2, sk, LANE), lambda b, h, i: (b, h, 0, 0))] * 2
        args += [k, v]
    tk_max = max(size for _, _, size in chunks)
    return pl.pallas_call(
        functools.partial(_attn_kernel, chunks=tuple(chunks)),
        out_shape=jax.ShapeDtypeStruct((bsz, sq, nh * MLA_V), BF16),
        grid=(bsz, nh // 2, sq // tq),
        in_specs=in_specs,
        out_specs=pl.BlockSpec((1, tq, LANE), lambda b, h, i: (b, i, h)),
        scratch_shapes=[pltpu.VMEM((2, 2, tq, tk_max), F32)],
        compiler_params=_cparams(("parallel", "parallel", "arbitrary")),
        name="mla_attention",
    )(*args)


def _dot_nt(a, b):
    return lax.dot_general(a, b, (((1,), (1,)), ((), ())), preferred_element_type=F32)


def _dot_tn(a, b):
    return lax.dot_general(a, b, (((0,), (0,)), ((), ())), preferred_element_type=F32)


def _split_bf16(x, n):
    parts = []
    for _ in range(n):
        p = x.astype(BF16)
        parts.append(p)
        x = x - p.astype(F32)
    return parts


def _dot_exact_lhs(a, x):
    hi, lo = _split_bf16(x, 2)
    return (jnp.dot(a, lo, preferred_element_type=F32)
            + jnp.dot(a, hi, preferred_element_type=F32))


_SCAN_COMBOS = (('gla', False), ('gla', True), ('ret', False), ('ret', True))


def _scans_kernel(*refs, nchunk, nbat):
    n = len(_SCAN_COMBOS)
    ins, outs, scr = refs[:7 * n], refs[7 * n:9 * n], refs[9 * n:10 * n]
    tab_it = iter(refs[10 * n:])
    tabs = [next(tab_it) if kind == 'ret' else None for kind, _ in _SCAN_COMBOS]
    step = pl.program_id(1)
    blk = nchunk * CHUNK

    @pl.when(step == 0)
    def _():
        for i, (kind, reverse) in enumerate(_SCAN_COMBOS):
            scr[i][...] = ins[7 * i + 6][...]
            if kind == 'ret':
                _ret_decay_tables(ins[7 * i + 5], tabs[i], reverse, blk)

    chains = []
    for r in range(nbat):
        for i, (kind, reverse) in enumerate(_SCAN_COMBOS):
            q_ref, k_ref, v_ref, x0_ref, x1_ref, x2_ref = ins[7 * i:7 * i + 6]
            if kind == 'gla':
                x0_ref = x0_ref.at[0, r]
            chains.append(_scan_chain(q_ref.at[0, r], k_ref.at[0, r], v_ref.at[0, r], x0_ref,
                                      x1_ref, x2_ref, outs[2 * i].at[0, r], scr[i].at[r],
                                      tabs[i], kind=kind, reverse=reverse, nchunk=nchunk))
    while chains:
        for chain in list(chains):
            if next(chain, 'done') == 'done':
                chains.remove(chain)

    @pl.when(step == pl.num_programs(1) - 1)
    def _():
        for i in range(n):
            outs[2 * i + 1][...] = scr[i][...]


def _chunk_masks(blk, reverse):
    ii = lax.broadcasted_iota(jnp.int32, (blk, blk), 0)
    jj = lax.broadcasted_iota(jnp.int32, (blk, blk), 1)
    same = (ii // CHUNK) == (jj // CHUNK)
    return same, (same & (jj > ii)) if reverse else (same & (ii >= jj))


def _ret_decay_tables(rd_ref, tab_ref, reverse, blk):
    ii = lax.broadcasted_iota(jnp.int32, (blk, blk), 0)
    jj = lax.broadcasted_iota(jnp.int32, (blk, blk), 1)
    _, amask = _chunk_masks(blk, reverse)
    dd = jnp.where(amask, (jj - ii if reverse else ii - jj).astype(F32), 0.0)
    pidx = (lax.broadcasted_iota(jnp.int32, (blk, LANE), 0) % CHUNK).astype(F32)
    for h in range(4):
        lg = -jnp.exp(rd_ref[h:h + 1, :])
        lgb = jnp.concatenate([lg] * (blk // LANE), axis=1) if blk > LANE else lg[:, :blk]
        tab_ref[h, :, :blk] = jnp.where(amask, jnp.exp(dd * lgb), 0.0)
        if reverse:
            zeta, xi = jnp.exp(pidx * lg), jnp.exp((CHUNK - pidx) * lg)
        else:
            zeta, xi = jnp.exp((CHUNK - 1 - pidx) * lg), jnp.exp((pidx + 1.0) * lg)
        tab_ref[h, :, blk:blk + LANE] = zeta
        tab_ref[h, :, blk + LANE:] = xi


def _scan_chain(q_ref, k_ref, v_ref, x0_ref, x1_ref, x2_ref, o_ref, st_scr, tab_ref, *,
                kind, reverse, nchunk):
    if kind == 'gla':
        r_ref, w2_ref, bg_ref = x0_ref, x1_ref, x2_ref
    else:
        cos_ref, sin_ref, rd_ref = x0_ref, x1_ref, x2_ref

    blk = nchunk * CHUNK
    ii = lax.broadcasted_iota(jnp.int32, (blk, blk), 0)
    jj = lax.broadcasted_iota(jnp.int32, (blk, blk), 1)
    same, amask = _chunk_masks(blk, reverse)
    order = range(nchunk - 1, -1, -1) if reverse else range(nchunk)
    vb = v_ref[...].astype(BF16)

    if kind == 'gla':
        pre = jnp.dot(r_ref[...].astype(BF16), w2_ref[...],
                      preferred_element_type=F32) + bg_ref[...]
        yield
        la = jax.nn.log_sigmoid(pre) * (1.0 / GLA_GATE_NORMALIZER)
        tri = (same & ((jj >= ii) if reverse else (ii >= jj))).astype(BF16)
        cum = _dot_exact_lhs(tri, la)
        yield
        totb = jnp.concatenate(
            [jnp.broadcast_to(cum[c * CHUNK + (0 if reverse else CHUNK - 1)][None, :],
                              (CHUNK, cum.shape[1])) for c in range(nchunk)], axis=0)
        k32 = k_ref[...].astype(F32)
        qd = (q_ref[...].astype(F32) * ((GLA_DK ** -0.5) * jnp.exp(cum))).astype(BF16)
        kd = (k32 * jnp.exp(-cum)).astype(BF16)
        kz = (k32 * jnp.exp(totb - cum)).astype(BF16)
        dtot = jnp.exp(totb)
    else:
        cs = cos_ref[...]
        sn = sin_ref[...]

    for h in range(4):
        cols = slice(h * LANE, (h + 1) * LANE)
        vh = vb[:, cols]
        if kind == 'gla':
            qh, kzh = qd[:, cols], kz[:, cols]
            att = jnp.where(amask, _dot_nt(qh, kd[:, cols]), 0.0)
            xi = None
        else:
            lg = -jnp.exp(rd_ref[h:h + 1, :])
            q = q_ref[:, cols].astype(F32)
            k = k_ref[:, cols].astype(F32)
            qh = (q * cs + pltpu.roll(q, LANE // 2, 1) * sn).astype(BF16)
            kr = (k * cs + pltpu.roll(k, LANE // 2, 1) * sn) * (RET_DK ** -0.5)
            xi = tab_ref[h, :, blk + LANE:]
            kzh = (kr * tab_ref[h, :, blk:blk + LANE]).astype(BF16)
            att = _dot_nt(qh, kr.astype(BF16)) * tab_ref[h, :, :blk]
            gch = jnp.exp(CHUNK * lg)
        yield
        o_intra = jnp.dot(att.astype(BF16), vh, preferred_element_type=F32)
        upd = [_dot_tn(vh[c * CHUNK:(c + 1) * CHUNK], kzh[c * CHUNK:(c + 1) * CHUNK])
               for c in range(nchunk)]
        yield
        st = st_scr[h]
        for c in order:
            rows = slice(c * CHUNK, (c + 1) * CHUNK)
            o = o_intra[rows] + (_dot_nt(qh[rows], st.astype(BF16)) if xi is None
                                 else _dot_nt(qh[rows], st.astype(BF16)) * xi[rows])
            decay = dtot[c * CHUNK:c * CHUNK + 1, cols] if kind == 'gla' else gch
            st = st * decay + upd[c]
            o_ref[rows, cols] = o
        st_scr[h] = st
        yield


def _scans(mg, lat, bsz, seq, states, gla_extra, ret_extra):
    blk = _pick(seq, 256, CHUNK)
    nb = seq // blk
    nbat = 2 if bsz % 2 == 0 else 1
    const = lambda b, s: (0, 0)
    sspec = pl.BlockSpec((nbat, 4, LANE, LANE), lambda b, s: (b, 0, 0, 0))
    mg4 = mg.reshape(bsz // nbat, nbat, seq, mg.shape[1])
    lat4 = lat.reshape(bsz // nbat, nbat, seq, lat.shape[1])
    in_specs, args, out_specs = [], [], []
    for (kind, reverse), s0 in zip(_SCAN_COMBOS, states):
        pos = (lambda s: nb - 1 - s) if reverse else (lambda s: s)
        col0 = 0 if kind == 'gla' else 4
        in_specs += [pl.BlockSpec((1, nbat, blk, 512),
                                  lambda b, s, c=col0 + j, pos=pos: (b, 0, pos(s), c))
                     for j in range(3)]
        args += [mg4, mg4, mg4]
        if kind == 'gla':
            in_specs += [pl.BlockSpec((1, nbat, blk, LANE), lambda b, s, pos=pos: (b, 0, pos(s), 3)),
                         pl.BlockSpec((LANE, 512), const), pl.BlockSpec((1, 512), const)]
            args += [lat4, *gla_extra[reverse]]
        else:
            in_specs += [pl.BlockSpec((blk, LANE), lambda b, s, pos=pos: (pos(s), 0)),
                         pl.BlockSpec((blk, LANE), lambda b, s, pos=pos: (pos(s), 0)),
                         pl.BlockSpec((SUBLANE, LANE), const)]
            args += [ret_extra[0], ret_extra[1], ret_extra[2][reverse]]
        in_specs.append(sspec)
        args.append(s0)
        out_specs += [pl.BlockSpec((1, nbat, blk, 512), lambda b, s, pos=pos: (b, 0, pos(s), 0)),
                      sspec]
    n = len(_SCAN_COMBOS)
    outs = pl.pallas_call(
        functools.partial(_scans_kernel, nchunk=blk // CHUNK, nbat=nbat),
        out_shape=(jax.ShapeDtypeStruct((bsz // nbat, nbat, seq, 512), F32),
                   jax.ShapeDtypeStruct((bsz, 4, LANE, LANE), F32)) * n,
        grid=(bsz // nbat, nb),
        in_specs=in_specs,
        out_specs=out_specs,
        scratch_shapes=([pltpu.VMEM((nbat, 4, LANE, LANE), F32)] * n
                        + [pltpu.VMEM((4, blk, blk + 2 * LANE), F32)
                           for kind, _ in _SCAN_COMBOS if kind == 'ret']),
        compiler_params=_cparams(("parallel", "arbitrary")),
        name="recurrent_scans",
    )(*args)
    return [o.reshape(bsz * seq, 512) for o in outs[0::2]], list(outs[1::2])


def _merge_kernel(ym_ref, ogf_ref, ogb_ref, orf_ref, orb_ref, gg_ref, gr_ref, g0_ref, g1_ref,
                  g2_ref, on_ref, bg_ref, wb_ref, wo_ref, h_ref, gt_ref, o_ref):
    def gated_head_norm(of_ref, ob_ref, g_ref, w, rows):
        parts = []
        for hd in range(4):
            cols = slice(hd * LANE, (hd + 1) * LANE)
            osum = of_ref[rows, cols] + ob_ref[rows, cols]
            y = osum * _rms(osum, LANE)
            if w is not None:
                y = y * w
            g = g_ref[rows, cols].astype(F32)
            parts.append((y * (g * jax.nn.sigmoid(g))).astype(BF16))
        return jnp.concatenate(parts, axis=1)

    def rows_chain(rows):
        ys = (ym_ref[rows, :], gated_head_norm(ogf_ref, ogb_ref, gg_ref, on_ref[...], rows),
              gated_head_norm(orf_ref, orb_ref, gr_ref, None, rows))
        ts = [jnp.dot(y, wb_ref[n], preferred_element_type=F32) for n, y in enumerate(ys)]
        yield
        u = None
        for n, g_ref in enumerate((g0_ref, g1_ref, g2_ref)):
            cols = slice(n * D_MODEL, (n + 1) * D_MODEL)
            t = jax.nn.sigmoid(g_ref[rows, :].astype(F32) + bg_ref[:, cols]) * ts[n]
            u = t if u is None else u + t
        out = jnp.dot(u.astype(BF16), wo_ref[...], preferred_element_type=F32)
        o_ref[rows, :] = h_ref[rows, :] + gt_ref[0] * out
        yield

    tm = h_ref.shape[0]
    sub = tm // 2 if tm % (2 * HALO) == 0 else tm
    chains = [rows_chain(slice(r0, r0 + sub)) for r0 in range(0, tm, sub)]
    while chains:
        for chain in list(chains):
            if next(chain, 'done') == 'done':
                chains.remove(chain)


def _merge(y_mla, scan_o, mg, on, bg, wb, wo, h, gt):
    r, d = h.shape
    g = gt.shape[0]
    rpg = r // g
    tm = _pick(rpg, 512)
    yspec = pl.BlockSpec((tm, BRANCH_W), lambda i: (i, 0))
    gspec = lambda c: pl.BlockSpec((tm, BRANCH_W), lambda i, c=c: (i, c))
    mspec = lambda c: pl.BlockSpec((tm, d), lambda i, c=c: (i, c))
    return pl.pallas_call(
        _merge_kernel,
        out_shape=jax.ShapeDtypeStruct((r, d), F32),
        grid=(r // tm,),
        in_specs=[yspec] * 5 + [gspec(3), gspec(7), mspec(4), mspec(5), mspec(6),
                  pl.BlockSpec((1, LANE), lambda i: (0, 0)),
                  pl.BlockSpec((1, GATE_W), lambda i: (0, 0)),
                  pl.BlockSpec((N_BRANCH, BRANCH_W, d), lambda i: (0, 0, 0)),
                  pl.BlockSpec((d, d), lambda i: (0, 0)),
                  pl.BlockSpec((tm, d), lambda i: (i, 0)),
                  pl.BlockSpec((1, 1, d), lambda i: (i * tm // rpg, 0, 0))],
        out_specs=pl.BlockSpec((tm, d), lambda i: (i, 0)),
        compiler_params=_cparams(("parallel",)),
        name="gated_merge",
    )(y_mla, *scan_o, mg, mg, mg, mg, mg, on, bg, wb, wo, h, gt)


def _ffn_kernel(hp_ref, h_ref, hn_ref, sh_ref, sc_ref, gt_ref, nw_ref, wg_ref, wu_ref, wdw_ref,
                bdw_ref, wo_ref, o_ref, act_scr, *, tm, seq, tf):
    i = pl.program_id(0)

    def norm_mod(x):
        y = x * lax.rsqrt(jnp.mean(x * x, axis=-1, keepdims=True) + EPS) * nw_ref[...]
        return y * (1.0 + sc_ref[0]) + sh_ref[0]

    row0 = i * tm
    prev_ok = (row0 % seq != 0).astype(F32)
    next_ok = ((row0 + tm) % seq != 0).astype(F32)

    a_mid = norm_mod(h_ref[...]).astype(BF16)
    a_ext = jnp.concatenate([(norm_mod(hp_ref[...]) * prev_ok).astype(BF16), a_mid,
                             (norm_mod(hn_ref[...]) * next_ok).astype(BF16)], axis=0)
    nchunk = wg_ref.shape[1] // tf

    def hidden(j):
        cs = slice(j * tf, (j + 1) * tf)
        gate = jnp.dot(a_ext, wg_ref[:, cs], preferred_element_type=F32)
        up = jnp.dot(a_mid, wu_ref[:, cs], preferred_element_type=F32)
        return gate, up

    for j in range(nchunk):
        cs = slice(j * tf, (j + 1) * tf)
        gate, up = hidden(j)
        g_prev = pltpu.roll(gate, 1, 0)[HALO:HALO + tm]
        g_next = pltpu.roll(gate, tm + 2 * HALO - 1, 0)[HALO:HALO + tm]
        conv = (g_prev * wdw_ref[0:1, cs] + gate[HALO:HALO + tm] * wdw_ref[1:2, cs]
                + g_next * wdw_ref[2:3, cs] + bdw_ref[:, cs])
        act_scr[:, cs] = (jax.nn.gelu(conv, approximate=True) * up).astype(BF16)
    out = jnp.dot(act_scr[...], wo_ref[...], preferred_element_type=F32)
    o_ref[...] = h_ref[...] + gt_ref[0] * out


def _ffn(h, shift, scale, gt, nw, wg, wu, wdw, bdw, wo, seq):
    r, d = h.shape
    g = shift.shape[0]
    rpg = r // g
    dff = wg.shape[1]
    tm = _pick(seq, 512, HALO)
    tf = _pick(dff, 2 * LANE, LANE)
    nsub = tm // HALO
    nrb = r // HALO
    gidx = lambda i: (i * tm // rpg, 0, 0)
    const = lambda i: (0, 0)
    kern = functools.partial(_ffn_kernel, tm=tm, seq=seq, tf=tf)
    return pl.pallas_call(
        kern,
        out_shape=jax.ShapeDtypeStruct((r, d), F32),
        grid=(r // tm,),
        in_specs=[pl.BlockSpec((HALO, d), lambda i: (jnp.maximum(i * nsub - 1, 0), 0)),
                  pl.BlockSpec((tm, d), lambda i: (i, 0)),
                  pl.BlockSpec((HALO, d), lambda i: (jnp.minimum((i + 1) * nsub, nrb - 1), 0)),
                  pl.BlockSpec((1, 1, d), gidx), pl.BlockSpec((1, 1, d), gidx),
                  pl.BlockSpec((1, 1, d), gidx),
                  pl.BlockSpec((1, d), const),
                  pl.BlockSpec((d, dff), const),
                  pl.BlockSpec((d, dff), const),
                  pl.BlockSpec((3, dff), const),
                  pl.BlockSpec((1, dff), const),
                  pl.BlockSpec((dff, d), const)],
        out_specs=pl.BlockSpec((tm, d), lambda i: (i, 0)),
        scratch_shapes=[pltpu.VMEM((tm, dff), BF16)],
        compiler_params=_cparams(("parallel",)),
        name="conv_ffn",
    )(h, h, h, shift, scale, gt, nw, wg, wu, wdw, bdw, wo)


def _mla_weight_layout():
    pidx, psgn = _rope_partner(MLA_ROPE)
    qa_idx, qa_sgn, qb_idx, qb_sgn = [], [], [], []
    k_idx, k_sgn, v_idx, v_sgn = [], [], [], []
    for h in range(MLA_HEADS):
        base = h * MLA_QK
        qa_idx += list(range(base, base + MLA_QK)) + [0] * 32
        qa_sgn += [1.0] * MLA_QK + [0.0] * 32
        qb_idx += [0] * 64 + (base + MLA_NOPE + pidx).tolist() + [0] * 32
        qb_sgn += [0.0] * 64 + psgn.tolist() + [0.0] * 32
        kb = h * (MLA_NOPE + MLA_V)
        k_idx += list(range(kb, kb + MLA_NOPE)) + [0] * 64
        k_sgn += [1.0] * 64 + [0.0] * 64
        vcols = list(range(kb + MLA_NOPE, kb + MLA_NOPE + MLA_V))
        if h % 2 == 0:
            v_idx += vcols + [0] * 64
            v_sgn += [1.0] * 64 + [0.0] * 64
        else:
            v_idx += [0] * 64 + vcols
            v_sgn += [0.0] * 64 + [1.0] * 64
    f = lambda a, t: np.asarray(a, t)
    return (f(qa_idx + qb_idx, np.int32), f(qa_sgn + qb_sgn, np.float32),
            f(k_idx, np.int32), f(k_sgn, np.float32), f(v_idx, np.int32), f(v_sgn, np.float32))


def _slot_vec(w):
    pidx, _ = _rope_partner(MLA_ROPE)
    z32 = jnp.zeros((32,), F32)
    a = jnp.concatenate([w, z32])
    p = jnp.concatenate([jnp.zeros((64,), F32), w[MLA_NOPE + pidx], z32])
    return a.reshape(1, LANE), p.reshape(1, LANE)


def _mla_tables(seq, rope):
    ones = jnp.ones((seq, 64), F32)
    if not rope:
        return jnp.ones((seq, LANE), F32), jnp.zeros((seq, LANE), F32)
    pos = jnp.arange(seq)
    dim = MLA_ROPE // 2
    inv = ROPE_THETA ** (-jnp.arange(dim // 2, dtype=F32) * 2.0 / dim)
    ar = (pos // GRID_W).astype(F32)[:, None] * inv[None, :]
    ac = (pos % GRID_W).astype(F32)[:, None] * inv[None, :]
    cos = jnp.concatenate([ones, jnp.cos(ar), jnp.cos(ar), jnp.cos(ac), jnp.cos(ac),
                           jnp.ones((seq, 32), F32)], axis=1)
    sin = jnp.concatenate([0.0 * ones, jnp.sin(ar), jnp.sin(ar), jnp.sin(ac), jnp.sin(ac),
                           jnp.zeros((seq, 32), F32)], axis=1)
    return cos, sin


def _ret_tables(start, seq):
    inv = 1.0 / (RET_THETA ** jnp.linspace(0.0, 1.0, RET_DK // 2, dtype=F32))
    ang = (start + jnp.arange(seq)).astype(F32)[:, None] * inv[None, :]
    cos, sin = jnp.cos(ang), jnp.sin(ang)
    return jnp.concatenate([cos, cos], axis=1), jnp.concatenate([-sin, sin], axis=1)


def _split_in_proj(w):
    o = _IN_OFF
    z = lambda n: jnp.zeros((w.shape[0], n), w.dtype)
    kr = o['mla_kr']
    q8 = MLA_ROPE // 4
    partner = []
    for base in (kr, kr + 2 * q8):
        partner += [-w[:, base + q8:base + 2 * q8], w[:, base:base + q8]]
    w_lat = jnp.concatenate(
        [w[:, :kr], z(64), w[:, kr:kr + MLA_ROPE], w[:, o['gla_rf']:o['gla_rf'] + 2 * GLA_GATE_RANK],
         z(64)] + partner + [z(32)], axis=1)
    w_mg = jnp.concatenate([w[:, o['gla_q']:o['gla_rf']], w[:, o['ret_q']:]], axis=1)
    return w_lat.astype(BF16), w_mg.astype(BF16)


def _mixers(lat, mg, bsz, seq, lw, mla_tabs, ret_tabs, states):
    qkv = _mla_prep(lat, mla_tabs[0], mla_tabs[1], lw['qna'], lw['kvna'], lw['wq2'], lw['wk'],
                    lw['wv'], lw['qn'], lw['qnp'], lw['kn'], lw['knp'], bsz, seq)
    scan_o, finals = _scans(mg, lat, bsz, seq, states,
                            {False: (lw['w2f'], lw['bgf']), True: (lw['w2b'], lw['bgb'])},
                            (ret_tabs[0], ret_tabs[1], {False: lw['rd_f'], True: lw['rd_b']}))
    return qkv, scan_o, finals


def kernel(x, c, ctx, c_ctx, w_ada, b_ada, norm1_w, norm2_w, w_in, b_gate, mla_q_norm_a, mla_w_qb,
           mla_kv_norm_a, mla_w_kvb, mla_q_norm, mla_k_norm, gla_w_gk2, gla_b_gk, gla_o_norm,
           ret_decay, w_branch, w_out, w_ffn_in, w_dw, b_dw, w_ffn_out):
    bsz, seq, d = x.shape
    clen = ctx.shape[1]
    depth = w_ada.shape[0]
    r_lat, r_ctx = bsz * seq, bsz * clen

    npad = -(bsz + 1) % SUBLANE
    cc = jnp.concatenate([c, c_ctx[None, :], jnp.zeros((npad, d), F32)], axis=0)
    mod_all = _ada_mod(cc, w_ada, b_ada)

    qidx, qsgn, kidx, ksgn, vidx, vsgn = _mla_weight_layout()
    mla_lat_tabs = _mla_tables(seq, True)
    mla_ctx_tabs = _mla_tables(clen, False)
    ret_lat_tabs = _ret_tables(clen, seq)
    ret_ctx_tabs = _ret_tables(0, clen)
    zstate = jnp.zeros((bsz, 4, LANE, LANE), F32)

    h = x.reshape(r_lat, d)
    hc = ctx.reshape(r_ctx, d)
    for l in range(depth):
        need_ctx = l < depth - 1
        mods = mod_all[l].reshape(-1, 6, d)
        ml = [mods[:bsz, j].reshape(bsz, 1, d) for j in range(6)]
        mc = [mods[bsz:bsz + 1, j].reshape(1, 1, d) for j in range(6)]

        w_lat, w_mg = _split_in_proj(w_in[l])
        qn, qnp = _slot_vec(mla_q_norm[l])
        kn, knp = _slot_vec(mla_k_norm[l])
        w2 = gla_w_gk2[l]
        zr = jnp.zeros((LANE - 2 * GLA_GATE_RANK, 4 * GLA_DK), F32)
        z16 = jnp.zeros((GLA_GATE_RANK, 4 * GLA_DK), F32)
        lw = dict(
            qna=mla_q_norm_a[l].reshape(1, -1), kvna=mla_kv_norm_a[l].reshape(1, -1),
            wq2=(mla_w_qb[l][:, qidx] * qsgn[None, :]).astype(BF16),
            wk=(mla_w_kvb[l][:, kidx] * ksgn[None, :]).astype(BF16),
            wv=(mla_w_kvb[l][:, vidx] * vsgn[None, :]).astype(BF16),
            qn=qn, qnp=qnp, kn=kn, knp=knp,
            w2f=jnp.concatenate([zr, w2[0], z16], axis=0).astype(BF16),
            w2b=jnp.concatenate([zr, z16, w2[1]], axis=0).astype(BF16),
            bgf=gla_b_gk[l, 0].reshape(1, -1), bgb=gla_b_gk[l, 1].reshape(1, -1),
            gla_on=gla_o_norm[l].reshape(1, LANE),
            rd_f=jnp.broadcast_to(jnp.pad(ret_decay[l, 0], (0, 4))[:, None], (SUBLANE, LANE)),
            rd_b=jnp.broadcast_to(jnp.pad(ret_decay[l, 1], (0, 4))[:, None], (SUBLANE, LANE)),
        )
        nw1 = norm1_w[l].reshape(1, d)
        nw2 = norm2_w[l].reshape(1, d)
        bg = b_gate[l].reshape(1, GATE_W)
        wb = w_branch[l].astype(BF16)
        wo = w_out[l].astype(BF16)
        wg = w_ffn_in[l][:, :D_FF].astype(BF16)
        wu = w_ffn_in[l][:, D_FF:].astype(BF16)
        wfo = w_ffn_out[l].astype(BF16)
        bdw = b_dw[l].reshape(1, D_FF)

        def project(hh, m, name):
            return _in_proj(hh, m[0], m[1], nw1, w_lat, w_mg, 2048, 1024, name)

        lat_c, mg_c = project(hc, mc, "in_proj_ctx")
        (q_c, k_c, v_c), so_c, states = _mixers(
            lat_c, mg_c, bsz, clen, lw, mla_ctx_tabs, ret_ctx_tabs, (zstate,) * 4)
        lat_l, mg_l = project(h, ml, "in_proj")
        (q_l, k_l, v_l), so_l, _ = _mixers(
            lat_l, mg_l, bsz, seq, lw, mla_lat_tabs, ret_lat_tabs, states)
        y_mla = _attention(q_l, ((k_c, v_c), (k_l, v_l))).reshape(r_lat, BRANCH_W)
        h = _merge(y_mla, so_l, mg_l, lw['gla_on'], bg, wb, wo, h, ml[2])
        h = _ffn(h, ml[3], ml[4], ml[5], nw2, wg, wu, w_dw[l], bdw, wfo, seq)
        if need_ctx:
            ym_c = _attention(q_c, ((k_c, v_c),)).reshape(r_ctx, BRANCH_W)
            hc = _merge(ym_c, so_c, mg_c, lw['gla_on'], bg, wb, wo, hc, mc[2])
            hc = _ffn(hc, mc[3], mc[4], mc[5], nw2, wg, wu, w_dw[l], bdw, wfo, clen)
    return h.reshape(bsz, seq, d)
```

```python
import functools
import math

import numpy as np
import jax
import jax.numpy as jnp
from jax import lax
from jax.experimental import pallas as pl
from jax.experimental.pallas import tpu as pltpu

F32 = jnp.float32
BF16 = jnp.bfloat16

D_MODEL = 1024
GRID_W = 64
N_BRANCH = 3
BRANCH_W = 512
MLA_HEADS = 8
MLA_NOPE = 64
MLA_ROPE = 32
MLA_QK = MLA_NOPE + MLA_ROPE
MLA_V = BRANCH_W // MLA_HEADS
MLA_Q_LORA = 256
MLA_KV_LORA = 128
GLA_DK = 128
GLA_GATE_RANK = 16
GLA_GATE_NORMALIZER = 16.0
RET_DK = 128
D_FF = 2816
CHUNK = 64
ROPE_THETA = 10000.0
RET_THETA = 10000.0
EPS = 1e-6
LOG2E = math.log2(math.e)
LANE = 128
SUBLANE = 8
HALO = 16
VMEM_LIMIT = 56 * 1024 * 1024

_IN_WIDTHS = (('mla_q', 256), ('mla_kv', 128), ('mla_kr', 32), ('gla_q', 512), ('gla_k', 512),
              ('gla_v', 512), ('gla_g', 512), ('gla_rf', 16), ('gla_rb', 16), ('ret_q', 512),
              ('ret_k', 512), ('ret_v', 512), ('ret_g', 512), ('gate_mla', 1024),
              ('gate_gla', 1024), ('gate_ret', 1024))
_IN_OFF = {}
_o = 0
for _n, _w in _IN_WIDTHS:
    _IN_OFF[_n] = _o
    _o += _w
GATE_W = 3 * D_MODEL


def _cparams(sem):
    return pltpu.CompilerParams(dimension_semantics=sem, vmem_limit_bytes=VMEM_LIMIT)


def _pick(n, pref, mult=SUBLANE):
    if n <= pref:
        return n
    for t in range(pref - pref % mult, 0, -mult):
        if n % t == 0:
            return t
    return n


def _rope_partner(nrot):
    half = nrot // 2
    q = half // 2
    idx = np.zeros(nrot, np.int32)
    sgn = np.zeros(nrot, np.float32)
    for base in (0, half):
        for i in range(q):
            idx[base + i] = base + i + q
            sgn[base + i] = -1.0
            idx[base + q + i] = base + i
            sgn[base + q + i] = 1.0
    return idx, sgn


def _ada_kernel(c_ref, w_ref, b_ref, o_ref):
    c = c_ref[...]
    cs = c * jax.nn.sigmoid(c)
    o_ref[0] = jnp.dot(cs, w_ref[0], preferred_element_type=F32,
                       precision=lax.Precision.HIGHEST) + b_ref[0]


def _ada_mod(cc, w_ada, b_ada):
    nl, d, n = w_ada.shape
    tn = _pick(n, 1536, LANE)
    return pl.pallas_call(
        _ada_kernel,
        out_shape=jax.ShapeDtypeStruct((nl, cc.shape[0], n), F32),
        grid=(nl, n // tn),
        in_specs=[pl.BlockSpec((cc.shape[0], d), lambda l, j: (0, 0)),
                  pl.BlockSpec((1, d, tn), lambda l, j: (l, 0, j)),
                  pl.BlockSpec((1, 1, tn), lambda l, j: (l, 0, j))],
        out_specs=pl.BlockSpec((1, cc.shape[0], tn), lambda l, j: (l, 0, j)),
        compiler_params=_cparams(("arbitrary", "arbitrary")),
        name="ada_mod",
    )(cc, w_ada, b_ada.reshape(nl, 1, n))


def _in_proj_kernel(x_ref, sh_ref, sc_ref, nw_ref, wl_ref, w_ref, lat_ref, o_ref, a_scr):
    @pl.when(pl.program_id(1) == 0)
    def _():
        x = x_ref[...]
        y = x * lax.rsqrt(jnp.mean(x * x, axis=-1, keepdims=True) + EPS) * nw_ref[...]
        a = (y * (1.0 + sc_ref[0]) + sh_ref[0]).astype(BF16)
        a_scr[...] = a
        lat_ref[...] = jnp.dot(a, wl_ref[...], preferred_element_type=F32)

    o_ref[...] = jnp.dot(a_scr[...], w_ref[...], preferred_element_type=F32).astype(o_ref.dtype)


def _in_proj(x, shift, scale, nw, w_lat, w_mg, tm, tn, name):
    r, d = x.shape
    g = shift.shape[0]
    rpg = r // g
    n = w_mg.shape[1]
    nl = w_lat.shape[1]
    tm = _pick(rpg, tm)
    tn = _pick(n, tn, LANE)
    return pl.pallas_call(
        _in_proj_kernel,
        out_shape=(jax.ShapeDtypeStruct((r, nl), F32), jax.ShapeDtypeStruct((r, n), BF16)),
        grid=(r // tm, n // tn),
        in_specs=[pl.BlockSpec((tm, d), lambda i, j: (i, 0)),
                  pl.BlockSpec((1, 1, d), lambda i, j: (i * tm // rpg, 0, 0)),
                  pl.BlockSpec((1, 1, d), lambda i, j: (i * tm // rpg, 0, 0)),
                  pl.BlockSpec((1, d), lambda i, j: (0, 0)),
                  pl.BlockSpec((d, nl), lambda i, j: (0, 0)),
                  pl.BlockSpec((d, tn), lambda i, j: (0, j))],
        out_specs=(pl.BlockSpec((tm, nl), lambda i, j: (i, 0)),
                   pl.BlockSpec((tm, tn), lambda i, j: (i, j))),
        scratch_shapes=[pltpu.VMEM((tm, d), BF16)],
        compiler_params=_cparams(("parallel", "arbitrary")),
        name=name,
    )(x, shift, scale, nw, w_lat, w_mg)


def _rms(x, n):
    return lax.rsqrt(jnp.sum(x * x, axis=-1, keepdims=True) * (1.0 / n) + EPS)


def _mla_prep_kernel(cq_ref, ckv_ref, kr_ref, krp_ref, cos_ref, sin_ref, qna_ref, kvna_ref,
                     wq_ref, wk_ref, wv_ref, qn_ref, qnp_ref, kn_ref, knp_ref,
                     q_out, k_out, v_out):
    cos = cos_ref[...]
    sin = sin_ref[...]
    cq = cq_ref[...]
    cqn = (cq * _rms(cq, MLA_Q_LORA) * qna_ref[...]).astype(BF16)
    qx = jnp.dot(cqn, wq_ref[...], preferred_element_type=F32)
    ckv = ckv_ref[...]
    ckvn = (ckv * _rms(ckv, MLA_KV_LORA) * kvna_ref[...]).astype(BF16)
    kx = jnp.dot(ckvn, wk_ref[...], preferred_element_type=F32)
    vx = jnp.dot(ckvn, wv_ref[...], preferred_element_type=F32)
    lane = lax.broadcasted_iota(jnp.int32, kr_ref.shape, 1)
    krm = jnp.where(lane < MLA_QK, kr_ref[...], 0.0)
    krp = krp_ref[...]
    ssq_kr = jnp.sum(krm * krm, axis=-1, keepdims=True)
    qc = qn_ref[...] * cos
    qs = qnp_ref[...] * sin
    kc = kn_ref[...] * cos
    ks = knp_ref[...] * sin
    hw = MLA_HEADS * LANE
    for h in range(MLA_HEADS):
        sl = slice(h * LANE, (h + 1) * LANE)
        xa = qx[:, sl]
        xb = qx[:, hw + h * LANE: hw + (h + 1) * LANE]
        r = _rms(xa, MLA_QK) * (MLA_QK ** -0.5 * LOG2E)
        q_out[0, h] = ((xa * qc + xb * qs) * r).astype(BF16)
        kn = kx[:, sl]
        rk = lax.rsqrt((jnp.sum(kn * kn, axis=-1, keepdims=True) + ssq_kr) * (1.0 / MLA_QK) + EPS)
        k_out[0, h] = (((kn + krm) * kc + krp * ks) * rk).astype(BF16)
        ones_lane = MLA_V if h % 2 == 0 else 0
        v_out[0, h] = jnp.where(lane == ones_lane, 1.0, vx[:, sl]).astype(BF16)


def _mla_prep(lat, cos_t, sin_t, qna, kvna, wq2, wk, wv, qn, qnp, kn, knp, bsz, seq):
    tm = _pick(seq, 512)
    nb = seq // tm
    hw = MLA_HEADS * LANE
    row = lambda b, s: b * nb + s
    const = lambda b, s: (0, 0)
    oshape = jax.ShapeDtypeStruct((bsz, MLA_HEADS, seq, LANE), BF16)
    ospec = pl.BlockSpec((1, MLA_HEADS, tm, LANE), lambda b, s: (b, 0, s, 0))
    return pl.pallas_call(
        _mla_prep_kernel,
        out_shape=(oshape, oshape, oshape),
        grid=(bsz, nb),
        in_specs=[pl.BlockSpec((tm, 2 * LANE), lambda b, s: (row(b, s), 0)),
                  pl.BlockSpec((tm, LANE), lambda b, s: (row(b, s), 2)),
                  pl.BlockSpec((tm, LANE), lambda b, s: (row(b, s), 3)),
                  pl.BlockSpec((tm, LANE), lambda b, s: (row(b, s), 4)),
                  pl.BlockSpec((tm, LANE), lambda b, s: (s, 0)),
                  pl.BlockSpec((tm, LANE), lambda b, s: (s, 0)),
                  pl.BlockSpec((1, MLA_Q_LORA), const),
                  pl.BlockSpec((1, MLA_KV_LORA), const),
                  pl.BlockSpec((MLA_Q_LORA, 2 * hw), const),
                  pl.BlockSpec((MLA_KV_LORA, hw), const),
                  pl.BlockSpec((MLA_KV_LORA, hw), const),
                  pl.BlockSpec((1, LANE), const), pl.BlockSpec((1, LANE), const),
                  pl.BlockSpec((1, LANE), const), pl.BlockSpec((1, LANE), const)],
        out_specs=(ospec, ospec, ospec),
        compiler_params=_cparams(("parallel", "arbitrary")),
        name="mla_prep",
    )(lat, lat, lat, lat, cos_t, sin_t, qna, kvna, wq2, wk, wv, qn, qnp, kn, knp)


def _attn_kernel(*refs, chunks):
    q_ref, kv, o_ref, s_scr = refs[0], refs[1:-2], refs[-2], refs[-1]
    tq = q_ref.shape[2]
    accs = []
    for hh in range(2):
        q = q_ref[0, hh]

        def scores(c, hh=hh, q=q):
            src, start, size = chunks[c]
            s = _dot_nt(q, kv[2 * src][0, hh, start:start + size, :])
            s_scr[hh, c % 2, :, :size] = s
            return jnp.max(s, axis=-1, keepdims=True)

        m = jnp.full((tq, 1), -jnp.inf, F32)
        acc = jnp.zeros((tq, LANE), F32)
        mt = scores(0)
        for c, (src, start, size) in enumerate(chunks):
            mt_next = scores(c + 1) if c + 1 < len(chunks) else None
            m_new = jnp.maximum(m, mt)
            p = jnp.exp2(s_scr[hh, c % 2, :, :size] - m_new)
            acc = jnp.exp2(m - m_new) * acc + jnp.dot(
                p.astype(BF16), kv[2 * src + 1][0, hh, start:start + size, :],
                preferred_element_type=F32)
            m, mt = m_new, mt_next
        accs.append(acc)
    lane = lax.broadcasted_iota(jnp.int32, (tq, LANE), 1)
    inv0 = 1.0 / accs[0][:, MLA_V:MLA_V + 1]
    inv1 = 1.0 / accs[1][:, 0:1]
    o_ref[0] = jnp.where(lane < MLA_V, accs[0] * inv0, accs[1] * inv1).astype(o_ref.dtype)


def _attention(q, kvs):
    bsz, nh, sq, _ = q.shape
    tq = _pick(sq, 512)
    chunks, in_specs, args = [], [pl.BlockSpec((1, 2, tq, LANE), lambda b, h, i: (b, h, i, 0))], [q]
    for src, (k, v) in enumerate(kvs):
        sk = k.shape[2]
        tk = next(t for t in (2048, 1024, 512, 256, 128, sk) if sk % t == 0)
        chunks += [(src, start, tk) for start in range(0, sk, tk)]
        in_specs += [pl.BlockSpec((1, 2, sk, LANE), lambda b, h, i: (b, h, 0, 0))] * 2
        args += [k, v]
    tk_max = max(size for _, _, size in chunks)
    return pl.pallas_call(
        functools.partial(_attn_kernel, chunks=tuple(chunks)),
        out_shape=jax.ShapeDtypeStruct((bsz, sq, nh * MLA_V), BF16),
        grid=(bsz, nh // 2, sq // tq),
        in_specs=in_specs,
        out_specs=pl.BlockSpec((1, tq, LANE), lambda b, h, i: (b, i, h)),
        scratch_shapes=[pltpu.VMEM((2, 2, tq, tk_max), F32)],
        compiler_params=_cparams(("parallel", "parallel", "arbitrary")),
        name="mla_attention",
    )(*args)


def _dot_nt(a, b):
    return lax.dot_general(a, b, (((1,), (1,)), ((), ())), preferred_element_type=F32)


def _dot_tn(a, b):
    return lax.dot_general(a, b, (((0,), (0,)), ((), ())), preferred_element_type=F32)


def _split_bf16(x, n):
    parts = []
    for _ in range(n):
        p = x.astype(BF16)
        parts.append(p)
        x = x - p.astype(F32)
    return parts


def _dot_exact_lhs(a, x):
    hi, lo = _split_bf16(x, 2)
    return (jnp.dot(a, lo, preferred_element_type=F32)
            + jnp.dot(a, hi, preferred_element_type=F32))


_SCAN_COMBOS = (('gla', False), ('gla', True), ('ret', False), ('ret', True))


def _scans_kernel(*refs, nchunk, nbat):
    n = len(_SCAN_COMBOS)
    ins, outs, scr = refs[:7 * n], refs[7 * n:9 * n], refs[9 * n:10 * n]
    tab_it = iter(refs[10 * n:])
    tabs = [next(tab_it) if kind == 'ret' else None for kind, _ in _SCAN_COMBOS]
    step = pl.program_id(1)
    blk = nchunk * CHUNK

    @pl.when(step == 0)
    def _():
        for i, (kind, reverse) in enumerate(_SCAN_COMBOS):
            scr[i][...] = ins[7 * i + 6][...]
            if kind == 'ret':
                _ret_decay_tables(ins[7 * i + 5], tabs[i], reverse, blk)

    chains = []
    for r in range(nbat):
        for i, (kind, reverse) in enumerate(_SCAN_COMBOS):
            q_ref, k_ref, v_ref, x0_ref, x1_ref, x2_ref = ins[7 * i:7 * i + 6]
            if kind == 'gla':
                x0_ref = x0_ref.at[0, r]
            chains.append(_scan_chain(q_ref.at[0, r], k_ref.at[0, r], v_ref.at[0, r], x0_ref,
                                      x1_ref, x2_ref, outs[2 * i].at[0, r], scr[i].at[r],
                                      tabs[i], kind=kind, reverse=reverse, nchunk=nchunk))
    while chains:
        for chain in list(chains):
            if next(chain, 'done') == 'done':
                chains.remove(chain)

    @pl.when(step == pl.num_programs(1) - 1)
    def _():
        for i in range(n):
            outs[2 * i + 1][...] = scr[i][...]


def _chunk_masks(blk, reverse):
    ii = lax.broadcasted_iota(jnp.int32, (blk, blk), 0)
    jj = lax.broadcasted_iota(jnp.int32, (blk, blk), 1)
    same = (ii // CHUNK) == (jj // CHUNK)
    return same, (same & (jj > ii)) if reverse else (same & (ii >= jj))


def _ret_decay_tables(rd_ref, tab_ref, reverse, blk):
    ii = lax.broadcasted_iota(jnp.int32, (blk, blk), 0)
    jj = lax.broadcasted_iota(jnp.int32, (blk, blk), 1)
    _, amask = _chunk_masks(blk, reverse)
    dd = jnp.where(amask, (jj - ii if reverse else ii - jj).astype(F32), 0.0)
    pidx = (lax.broadcasted_iota(jnp.int32, (blk, LANE), 0) % CHUNK).astype(F32)
    for h in range(4):
        lg = -jnp.exp(rd_ref[h:h + 1, :])
        lgb = jnp.concatenate([lg] * (blk // LANE), axis=1) if blk > LANE else lg[:, :blk]
        tab_ref[h, :, :blk] = jnp.where(amask, jnp.exp(dd * lgb), 0.0)
        if reverse:
            zeta, xi = jnp.exp(pidx * lg), jnp.exp((CHUNK - pidx) * lg)
        else:
            zeta, xi = jnp.exp((CHUNK - 1 - pidx) * lg), jnp.exp((pidx + 1.0) * lg)
        tab_ref[h, :, blk:blk + LANE] = zeta
        tab_ref[h, :, blk + LANE:] = xi


def _scan_chain(q_ref, k_ref, v_ref, x0_ref, x1_ref, x2_ref, o_ref, st_scr, tab_ref, *,
                kind, reverse, nchunk):
    if kind == 'gla':
        r_ref, w2_ref, bg_ref = x0_ref, x1_ref, x2_ref
    else:
        cos_ref, sin_ref, rd_ref = x0_ref, x1_ref, x2_ref

    blk = nchunk * CHUNK
    ii = lax.broadcasted_iota(jnp.int32, (blk, blk), 0)
    jj = lax.broadcasted_iota(jnp.int32, (blk, blk), 1)
    same, amask = _chunk_masks(blk, reverse)
    order = range(nchunk - 1, -1, -1) if reverse else range(nchunk)
    vb = v_ref[...].astype(BF16)

    if kind == 'gla':
        pre = jnp.dot(r_ref[...].astype(BF16), w2_ref[...],
                      preferred_element_type=F32) + bg_ref[...]
        yield
        la = jax.nn.log_sigmoid(pre) * (1.0 / GLA_GATE_NORMALIZER)
        tri = (same & ((jj >= ii) if reverse else (ii >= jj))).astype(BF16)
        cum = _dot_exact_lhs(tri, la)
        yield
        totb = jnp.concatenate(
            [jnp.broadcast_to(cum[c * CHUNK + (0 if reverse else CHUNK - 1)][None, :],
                              (CHUNK, cum.shape[1])) for c in range(nchunk)], axis=0)
        k32 = k_ref[...].astype(F32)
        qd = (q_ref[...].astype(F32) * ((GLA_DK ** -0.5) * jnp.exp(cum))).astype(BF16)
        kd = (k32 * jnp.exp(-cum)).astype(BF16)
        kz = (k32 * jnp.exp(totb - cum)).astype(BF16)
        dtot = jnp.exp(totb)
    else:
        cs = cos_ref[...]
        sn = sin_ref[...]

    for h in range(4):
        cols = slice(h * LANE, (h + 1) * LANE)
        vh = vb[:, cols]
        if kind == 'gla':
            qh, kzh = qd[:, cols], kz[:, cols]
            att = jnp.where(amask, _dot_nt(qh, kd[:, cols]), 0.0)
            xi = None
        else:
            lg = -jnp.exp(rd_ref[h:h + 1, :])
            q = q_ref[:, cols].astype(F32)
            k = k_ref[:, cols].astype(F32)
            qh = (q * cs + pltpu.roll(q, LANE // 2, 1) * sn).astype(BF16)
            kr = (k * cs + pltpu.roll(k, LANE // 2, 1) * sn) * (RET_DK ** -0.5)
            xi = tab_ref[h, :, blk + LANE:]
            kzh = (kr * tab_ref[h, :, blk:blk + LANE]).astype(BF16)
            att = _dot_nt(qh, kr.astype(BF16)) * tab_ref[h, :, :blk]
            gch = jnp.exp(CHUNK * lg)
        yield
        o_intra = jnp.dot(att.astype(BF16), vh, preferred_element_type=F32)
        upd = [_dot_tn(vh[c * CHUNK:(c + 1) * CHUNK], kzh[c * CHUNK:(c + 1) * CHUNK])
               for c in range(nchunk)]
        yield
        st = st_scr[h]
        for c in order:
            rows = slice(c * CHUNK, (c + 1) * CHUNK)
            o = o_intra[rows] + (_dot_nt(qh[rows], st.astype(BF16)) if xi is None
                                 else _dot_nt(qh[rows], st.astype(BF16)) * xi[rows])
            decay = dtot[c * CHUNK:c * CHUNK + 1, cols] if kind == 'gla' else gch
            st = st * decay + upd[c]
            o_ref[rows, cols] = o
        st_scr[h] = st
        yield


def _scans(mg, lat, bsz, seq, states, gla_extra, ret_extra):
    blk = _pick(seq, 256, CHUNK)
    nb = seq // blk
    nbat = 2 if bsz % 2 == 0 else 1
    const = lambda b, s: (0, 0)
    sspec = pl.BlockSpec((nbat, 4, LANE, LANE), lambda b, s: (b, 0, 0, 0))
    mg4 = mg.reshape(bsz // nbat, nbat, seq, mg.shape[1])
    lat4 = lat.reshape(bsz // nbat, nbat, seq, lat.shape[1])
    in_specs, args, out_specs = [], [], []
    for (kind, reverse), s0 in zip(_SCAN_COMBOS, states):
        pos = (lambda s: nb - 1 - s) if reverse else (lambda s: s)
        col0 = 0 if kind == 'gla' else 4
        in_specs += [pl.BlockSpec((1, nbat, blk, 512),
                                  lambda b, s, c=col0 + j, pos=pos: (b, 0, pos(s), c))
                     for j in range(3)]
        args += [mg4, mg4, mg4]
        if kind == 'gla':
            in_specs += [pl.BlockSpec((1, nbat, blk, LANE), lambda b, s, pos=pos: (b, 0, pos(s), 3)),
                         pl.BlockSpec((LANE, 512), const), pl.BlockSpec((1, 512), const)]
            args += [lat4, *gla_extra[reverse]]
        else:
            in_specs += [pl.BlockSpec((blk, LANE), lambda b, s, pos=pos: (pos(s), 0)),
                         pl.BlockSpec((blk, LANE), lambda b, s, pos=pos: (pos(s), 0)),
                         pl.BlockSpec((SUBLANE, LANE), const)]
            args += [ret_extra[0], ret_extra[1], ret_extra[2][reverse]]
        in_specs.append(sspec)
        args.append(s0)
        out_specs += [pl.BlockSpec((1, nbat, blk, 512), lambda b, s, pos=pos: (b, 0, pos(s), 0)),
                      sspec]
    n = len(_SCAN_COMBOS)
    outs = pl.pallas_call(
        functools.partial(_scans_kernel, nchunk=blk // CHUNK, nbat=nbat),
        out_shape=(jax.ShapeDtypeStruct((bsz // nbat, nbat, seq, 512), F32),
                   jax.ShapeDtypeStruct((bsz, 4, LANE, LANE), F32)) * n,
        grid=(bsz // nbat, nb),
        in_specs=in_specs,
        out_specs=out_specs,
        scratch_shapes=([pltpu.VMEM((nbat, 4, LANE, LANE), F32)] * n
                        + [pltpu.VMEM((4, blk, blk + 2 * LANE), F32)
                           for kind, _ in _SCAN_COMBOS if kind == 'ret']),
        compiler_params=_cparams(("parallel", "arbitrary")),
        name="recurrent_scans",
    )(*args)
    return [o.reshape(bsz * seq, 512) for o in outs[0::2]], list(outs[1::2])


def _merge_kernel(ym_ref, ogf_ref, ogb_ref, orf_ref, orb_ref, gg_ref, gr_ref, g0_ref, g1_ref,
                  g2_ref, on_ref, bg_ref, wb_ref, wo_ref, h_ref, gt_ref, o_ref):
    def gated_head_norm(of_ref, ob_ref, g_ref, w, rows):
        parts = []
        for hd in range(4):
            cols = slice(hd * LANE, (hd + 1) * LANE)
            osum = of_ref[rows, cols] + ob_ref[rows, cols]
            y = osum * _rms(osum, LANE)
            if w is not None:
                y = y * w
            g = g_ref[rows, cols].astype(F32)
            parts.append((y * (g * jax.nn.sigmoid(g))).astype(BF16))
        return jnp.concatenate(parts, axis=1)

    def rows_chain(rows):
        ys = (ym_ref[rows, :], gated_head_norm(ogf_ref, ogb_ref, gg_ref, on_ref[...], rows),
              gated_head_norm(orf_ref, orb_ref, gr_ref, None, rows))
        ts = [jnp.dot(y, wb_ref[n], preferred_element_type=F32) for n, y in enumerate(ys)]
        yield
        u = None
        for n, g_ref in enumerate((g0_ref, g1_ref, g2_ref)):
            cols = slice(n * D_MODEL, (n + 1) * D_MODEL)
            t = jax.nn.sigmoid(g_ref[rows, :].astype(F32) + bg_ref[:, cols]) * ts[n]
            u = t if u is None else u + t
        out = jnp.dot(u.astype(BF16), wo_ref[...], preferred_element_type=F32)
        o_ref[rows, :] = h_ref[rows, :] + gt_ref[0] * out
        yield

    tm = h_ref.shape[0]
    sub = tm // 2 if tm % (2 * HALO) == 0 else tm
    chains = [rows_chain(slice(r0, r0 + sub)) for r0 in range(0, tm, sub)]
    while chains:
        for chain in list(chains):
            if next(chain, 'done') == 'done':
                chains.remove(chain)


def _merge(y_mla, scan_o, mg, on, bg, wb, wo, h, gt):
    r, d = h.shape
    g = gt.shape[0]
    rpg = r // g
    tm = _pick(rpg, 512)
    yspec = pl.BlockSpec((tm, BRANCH_W), lambda i: (i, 0))
    gspec = lambda c: pl.BlockSpec((tm, BRANCH_W), lambda i, c=c: (i, c))
    mspec = lambda c: pl.BlockSpec((tm, d), lambda i, c=c: (i, c))
    return pl.pallas_call(
        _merge_kernel,
        out_shape=jax.ShapeDtypeStruct((r, d), F32),
        grid=(r // tm,),
        in_specs=[yspec] * 5 + [gspec(3), gspec(7), mspec(4), mspec(5), mspec(6),
                  pl.BlockSpec((1, LANE), lambda i: (0, 0)),
                  pl.BlockSpec((1, GATE_W), lambda i: (0, 0)),
                  pl.BlockSpec((N_BRANCH, BRANCH_W, d), lambda i: (0, 0, 0)),
                  pl.BlockSpec((d, d), lambda i: (0, 0)),
                  pl.BlockSpec((tm, d), lambda i: (i, 0)),
                  pl.BlockSpec((1, 1, d), lambda i: (i * tm // rpg, 0, 0))],
        out_specs=pl.BlockSpec((tm, d), lambda i: (i, 0)),
        compiler_params=_cparams(("parallel",)),
        name="gated_merge",
    )(y_mla, *scan_o, mg, mg, mg, mg, mg, on, bg, wb, wo, h, gt)


def _ffn_kernel(hp_ref, h_ref, hn_ref, sh_ref, sc_ref, gt_ref, nw_ref, wg_ref, wu_ref, wdw_ref,
                bdw_ref, wo_ref, o_ref, act_scr, *, tm, seq, tf):
    i = pl.program_id(0)

    def norm_mod(x):
        y = x * lax.rsqrt(jnp.mean(x * x, axis=-1, keepdims=True) + EPS) * nw_ref[...]
        return y * (1.0 + sc_ref[0]) + sh_ref[0]

    row0 = i * tm
    prev_ok = (row0 % seq != 0).astype(F32)
    next_ok = ((row0 + tm) % seq != 0).astype(F32)

    a_mid = norm_mod(h_ref[...]).astype(BF16)
    a_ext = jnp.concatenate([(norm_mod(hp_ref[...]) * prev_ok).astype(BF16), a_mid,
                             (norm_mod(hn_ref[...]) * next_ok).astype(BF16)], axis=0)
    nchunk = wg_ref.shape[1] // tf

    def hidden(j):
        cs = slice(j * tf, (j + 1) * tf)
        gate = jnp.dot(a_ext, wg_ref[:, cs], preferred_element_type=F32)
        up = jnp.dot(a_mid, wu_ref[:, cs], preferred_element_type=F32)
        return gate, up

    for j in range(nchunk):
        cs = slice(j * tf, (j + 1) * tf)
        gate, up = hidden(j)
        g_prev = pltpu.roll(gate, 1, 0)[HALO:HALO + tm]
        g_next = pltpu.roll(gate, tm + 2 * HALO - 1, 0)[HALO:HALO + tm]
        conv = (g_prev * wdw_ref[0:1, cs] + gate[HALO:HALO + tm] * wdw_ref[1:2, cs]
                + g_next * wdw_ref[2:3, cs] + bdw_ref[:, cs])
        act_scr[:, cs] = (jax.nn.gelu(conv, approximate=True) * up).astype(BF16)
    out = jnp.dot(act_scr[...], wo_ref[...], preferred_element_type=F32)
    o_ref[...] = h_ref[...] + gt_ref[0] * out


def _ffn(h, shift, scale, gt, nw, wg, wu, wdw, bdw, wo, seq):
    r, d = h.shape
    g = shift.shape[0]
    rpg = r // g
    dff = wg.shape[1]
    tm = _pick(seq, 512, HALO)
    tf = _pick(dff, 2 * LANE, LANE)
    nsub = tm // HALO
    nrb = r // HALO
    gidx = lambda i: (i * tm // rpg, 0, 0)
    const = lambda i: (0, 0)
    kern = functools.partial(_ffn_kernel, tm=tm, seq=seq, tf=tf)
    return pl.pallas_call(
        kern,
        out_shape=jax.ShapeDtypeStruct((r, d), F32),
        grid=(r // tm,),
        in_specs=[pl.BlockSpec((HALO, d), lambda i: (jnp.maximum(i * nsub - 1, 0), 0)),
                  pl.BlockSpec((tm, d), lambda i: (i, 0)),
                  pl.BlockSpec((HALO, d), lambda i: (jnp.minimum((i + 1) * nsub, nrb - 1), 0)),
                  pl.BlockSpec((1, 1, d), gidx), pl.BlockSpec((1, 1, d), gidx),
                  pl.BlockSpec((1, 1, d), gidx),
                  pl.BlockSpec((1, d), const),
                  pl.BlockSpec((d, dff), const),
                  pl.BlockSpec((d, dff), const),
                  pl.BlockSpec((3, dff), const),
                  pl.BlockSpec((1, dff), const),
                  pl.BlockSpec((dff, d), const)],
        out_specs=pl.BlockSpec((tm, d), lambda i: (i, 0)),
        scratch_shapes=[pltpu.VMEM((tm, dff), BF16)],
        compiler_params=_cparams(("parallel",)),
        name="conv_ffn",
    )(h, h, h, shift, scale, gt, nw, wg, wu, wdw, bdw, wo)


def _mla_weight_layout():
    pidx, psgn = _rope_partner(MLA_ROPE)
    qa_idx, qa_sgn, qb_idx, qb_sgn = [], [], [], []
    k_idx, k_sgn, v_idx, v_sgn = [], [], [], []
    for h in range(MLA_HEADS):
        base = h * MLA_QK
        qa_idx += list(range(base, base + MLA_QK)) + [0] * 32
        qa_sgn += [1.0] * MLA_QK + [0.0] * 32
        qb_idx += [0] * 64 + (base + MLA_NOPE + pidx).tolist() + [0] * 32
        qb_sgn += [0.0] * 64 + psgn.tolist() + [0.0] * 32
        kb = h * (MLA_NOPE + MLA_V)
        k_idx += list(range(kb, kb + MLA_NOPE)) + [0] * 64
        k_sgn += [1.0] * 64 + [0.0] * 64
        vcols = list(range(kb + MLA_NOPE, kb + MLA_NOPE + MLA_V))
        if h % 2 == 0:
            v_idx += vcols + [0] * 64
            v_sgn += [1.0] * 64 + [0.0] * 64
        else:
            v_idx += [0] * 64 + vcols
            v_sgn += [0.0] * 64 + [1.0] * 64
    f = lambda a, t: np.asarray(a, t)
    return (f(qa_idx + qb_idx, np.int32), f(qa_sgn + qb_sgn, np.float32),
            f(k_idx, np.int32), f(k_sgn, np.float32), f(v_idx, np.int32), f(v_sgn, np.float32))


def _slot_vec(w):
    pidx, _ = _rope_partner(MLA_ROPE)
    z32 = jnp.zeros((32,), F32)
    a = jnp.concatenate([w, z32])
    p = jnp.concatenate([jnp.zeros((64,), F32), w[MLA_NOPE + pidx], z32])
    return a.reshape(1, LANE), p.reshape(1, LANE)


def _mla_tables(seq, rope):
    ones = jnp.ones((seq, 64), F32)
    if not rope:
        return jnp.ones((seq, LANE), F32), jnp.zeros((seq, LANE), F32)
    pos = jnp.arange(seq)
    dim = MLA_ROPE // 2
    inv = ROPE_THETA ** (-jnp.arange(dim // 2, dtype=F32) * 2.0 / dim)
    ar = (pos // GRID_W).astype(F32)[:, None] * inv[None, :]
    ac = (pos % GRID_W).astype(F32)[:, None] * inv[None, :]
    cos = jnp.concatenate([ones, jnp.cos(ar), jnp.cos(ar), jnp.cos(ac), jnp.cos(ac),
                           jnp.ones((seq, 32), F32)], axis=1)
    sin = jnp.concatenate([0.0 * ones, jnp.sin(ar), jnp.sin(ar), jnp.sin(ac), jnp.sin(ac),
                           jnp.zeros((seq, 32), F32)], axis=1)
    return cos, sin


def _ret_tables(start, seq):
    inv = 1.0 / (RET_THETA ** jnp.linspace(0.0, 1.0, RET_DK // 2, dtype=F32))
    ang = (start + jnp.arange(seq)).astype(F32)[:, None] * inv[None, :]
    cos, sin = jnp.cos(ang), jnp.sin(ang)
    return jnp.concatenate([cos, cos], axis=1), jnp.concatenate([-sin, sin], axis=1)


def _split_in_proj(w):
    o = _IN_OFF
    z = lambda n: jnp.zeros((w.shape[0], n), w.dtype)
    kr = o['mla_kr']
    q8 = MLA_ROPE // 4
    partner = []
    for base in (kr, kr + 2 * q8):
        partner += [-w[:, base + q8:base + 2 * q8], w[:, base:base + q8]]
    w_lat = jnp.concatenate(
        [w[:, :kr], z(64), w[:, kr:kr + MLA_ROPE], w[:, o['gla_rf']:o['gla_rf'] + 2 * GLA_GATE_RANK],
         z(64)] + partner + [z(32)], axis=1)
    w_mg = jnp.concatenate([w[:, o['gla_q']:o['gla_rf']], w[:, o['ret_q']:]], axis=1)
    return w_lat.astype(BF16), w_mg.astype(BF16)


def _mixers(lat, mg, bsz, seq, lw, mla_tabs, ret_tabs, states):
    qkv = _mla_prep(lat, mla_tabs[0], mla_tabs[1], lw['qna'], lw['kvna'], lw['wq2'], lw['wk'],
                    lw['wv'], lw['qn'], lw['qnp'], lw['kn'], lw['knp'], bsz, seq)
    scan_o, finals = _scans(mg, lat, bsz, seq, states,
                            {False: (lw['w2f'], lw['bgf']), True: (lw['w2b'], lw['bgb'])},
                            (ret_tabs[0], ret_tabs[1], {False: lw['rd_f'], True: lw['rd_b']}))
    return qkv, scan_o, finals


def kernel(x, c, ctx, c_ctx, w_ada, b_ada, norm1_w, norm2_w, w_in, b_gate, mla_q_norm_a, mla_w_qb,
           mla_kv_norm_a, mla_w_kvb, mla_q_norm, mla_k_norm, gla_w_gk2, gla_b_gk, gla_o_norm,
           ret_decay, w_branch, w_out, w_ffn_in, w_dw, b_dw, w_ffn_out):
    bsz, seq, d = x.shape
    clen = ctx.shape[1]
    depth = w_ada.shape[0]
    r_lat, r_ctx = bsz * seq, bsz * clen

    npad = -(bsz + 1) % SUBLANE
    cc = jnp.concatenate([c, c_ctx[None, :], jnp.zeros((npad, d), F32)], axis=0)
    mod_all = _ada_mod(cc, w_ada, b_ada)

    qidx, qsgn, kidx, ksgn, vidx, vsgn = _mla_weight_layout()
    mla_lat_tabs = _mla_tables(seq, True)
    mla_ctx_tabs = _mla_tables(clen, False)
    ret_lat_tabs = _ret_tables(clen, seq)
    ret_ctx_tabs = _ret_tables(0, clen)
    zstate = jnp.zeros((bsz, 4, LANE, LANE), F32)

    h = x.reshape(r_lat, d)
    hc = ctx.reshape(r_ctx, d)
    for l in range(depth):
        need_ctx = l < depth - 1
        mods = mod_all[l].reshape(-1, 6, d)
        ml = [mods[:bsz, j].reshape(bsz, 1, d) for j in range(6)]
        mc = [mods[bsz:bsz + 1, j].reshape(1, 1, d) for j in range(6)]

        w_lat, w_mg = _split_in_proj(w_in[l])
        qn, qnp = _slot_vec(mla_q_norm[l])
        kn, knp = _slot_vec(mla_k_norm[l])
        w2 = gla_w_gk2[l]
        zr = jnp.zeros((LANE - 2 * GLA_GATE_RANK, 4 * GLA_DK), F32)
        z16 = jnp.zeros((GLA_GATE_RANK, 4 * GLA_DK), F32)
        lw = dict(
            qna=mla_q_norm_a[l].reshape(1, -1), kvna=mla_kv_norm_a[l].reshape(1, -1),
            wq2=(mla_w_qb[l][:, qidx] * qsgn[None, :]).astype(BF16),
            wk=(mla_w_kvb[l][:, kidx] * ksgn[None, :]).astype(BF16),
            wv=(mla_w_kvb[l][:, vidx] * vsgn[None, :]).astype(BF16),
            qn=qn, qnp=qnp, kn=kn, knp=knp,
            w2f=jnp.concatenate([zr, w2[0], z16], axis=0).astype(BF16),
            w2b=jnp.concatenate([zr, z16, w2[1]], axis=0).astype(BF16),
            bgf=gla_b_gk[l, 0].reshape(1, -1), bgb=gla_b_gk[l, 1].reshape(1, -1),
            gla_on=gla_o_norm[l].reshape(1, LANE),
            rd_f=jnp.broadcast_to(jnp.pad(ret_decay[l, 0], (0, 4))[:, None], (SUBLANE, LANE)),
            rd_b=jnp.broadcast_to(jnp.pad(ret_decay[l, 1], (0, 4))[:, None], (SUBLANE, LANE)),
        )
        nw1 = norm1_w[l].reshape(1, d)
        nw2 = norm2_w[l].reshape(1, d)
        bg = b_gate[l].reshape(1, GATE_W)
        wb = w_branch[l].astype(BF16)
        wo = w_out[l].astype(BF16)
        wg = w_ffn_in[l][:, :D_FF].astype(BF16)
        wu = w_ffn_in[l][:, D_FF:].astype(BF16)
        wfo = w_ffn_out[l].astype(BF16)
        bdw = b_dw[l].reshape(1, D_FF)

        def project(hh, m, name):
            return _in_proj(hh, m[0], m[1], nw1, w_lat, w_mg, 2048, 1024, name)

        lat_c, mg_c = project(hc, mc, "in_proj_ctx")
        (q_c, k_c, v_c), so_c, states = _mixers(
            lat_c, mg_c, bsz, clen, lw, mla_ctx_tabs, ret_ctx_tabs, (zstate,) * 4)
        lat_l, mg_l = project(h, ml, "in_proj")
        (q_l, k_l, v_l), so_l, _ = _mixers(
            lat_l, mg_l, bsz, seq, lw, mla_lat_tabs, ret_lat_tabs, states)
        y_mla = _attention(q_l, ((k_c, v_c), (k_l, v_l))).reshape(r_lat, BRANCH_W)
        h = _merge(y_mla, so_l, mg_l, lw['gla_on'], bg, wb, wo, h, ml[2])
        h = _ffn(h, ml[3], ml[4], ml[5], nw2, wg, wu, w_dw[l], bdw, wfo, seq)
        if need_ctx:
            ym_c = _attention(q_c, ((k_c, v_c),)).reshape(r_ctx, BRANCH_W)
            hc = _merge(ym_c, so_c, mg_c, lw['gla_on'], bg, wb, wo, hc, mc[2])
            hc = _ffn(hc, mc[3], mc[4], mc[5], nw2, wg, wu, w_dw[l], bdw, wfo, clen)
    return h.reshape(bsz, seq, d)
```

```python
import functools
import math

import numpy as np
import jax
import jax.numpy as jnp
from jax import lax
from jax.experimental import pallas as pl
from jax.experimental.pallas import tpu as pltpu

F32 = jnp.float32
BF16 = jnp.bfloat16

D_MODEL = 1024
GRID_W = 64
N_BRANCH = 3
BRANCH_W = 512
MLA_HEADS = 8
MLA_NOPE = 64
MLA_ROPE = 32
MLA_QK = MLA_NOPE + MLA_ROPE
MLA_V = BRANCH_W // MLA_HEADS
MLA_Q_LORA = 256
MLA_KV_LORA = 128
GLA_DK = 128
GLA_GATE_RANK = 16
GLA_GATE_NORMALIZER = 16.0
RET_DK = 128
D_FF = 2816
CHUNK = 64
ROPE_THETA = 10000.0
RET_THETA = 10000.0
EPS = 1e-6
LOG2E = math.log2(math.e)
LANE = 128
SUBLANE = 8
HALO = 16
VMEM_LIMIT = 56 * 1024 * 1024

_IN_WIDTHS = (('mla_q', 256), ('mla_kv', 128), ('mla_kr', 32), ('gla_q', 512), ('gla_k', 512),
              ('gla_v', 512), ('gla_g', 512), ('gla_rf', 16), ('gla_rb', 16), ('ret_q', 512),
              ('ret_k', 512), ('ret_v', 512), ('ret_g', 512), ('gate_mla', 1024),
              ('gate_gla', 1024), ('gate_ret', 1024))
_IN_OFF = {}
_o = 0
for _n, _w in _IN_WIDTHS:
    _IN_OFF[_n] = _o
    _o += _w
GATE_W = 3 * D_MODEL


def _cparams(sem):
    return pltpu.CompilerParams(dimension_semantics=sem, vmem_limit_bytes=VMEM_LIMIT)


def _pick(n, pref, mult=SUBLANE):
    if n <= pref:
        return n
    for t in range(pref - pref % mult, 0, -mult):
        if n % t == 0:
            return t
    return n


def _rope_partner(nrot):
    half = nrot // 2
    q = half // 2
    idx = np.zeros(nrot, np.int32)
    sgn = np.zeros(nrot, np.float32)
    for base in (0, half):
        for i in range(q):
            idx[base + i] = base + i + q
            sgn[base + i] = -1.0
            idx[base + q + i] = base + i
            sgn[base + q + i] = 1.0
    return idx, sgn


def _ada_kernel(c_ref, w_ref, b_ref, o_ref):
    c = c_ref[...]
    cs = c * jax.nn.sigmoid(c)
    o_ref[0] = jnp.dot(cs, w_ref[0], preferred_element_type=F32,
                       precision=lax.Precision.HIGHEST) + b_ref[0]


def _ada_mod(cc, w_ada, b_ada):
    nl, d, n = w_ada.shape
    tn = _pick(n, 1536, LANE)
    return pl.pallas_call(
        _ada_kernel,
        out_shape=jax.ShapeDtypeStruct((nl, cc.shape[0], n), F32),
        grid=(nl, n // tn),
        in_specs=[pl.BlockSpec((cc.shape[0], d), lambda l, j: (0, 0)),
                  pl.BlockSpec((1, d, tn), lambda l, j: (l, 0, j)),
                  pl.BlockSpec((1, 1, tn), lambda l, j: (l, 0, j))],
        out_specs=pl.BlockSpec((1, cc.shape[0], tn), lambda l, j: (l, 0, j)),
        compiler_params=_cparams(("arbitrary", "arbitrary")),
        name="ada_mod",
    )(cc, w_ada, b_ada.reshape(nl, 1, n))


def _in_proj_kernel(x_ref, sh_ref, sc_ref, nw_ref, wl_ref, w_ref, lat_ref, o_ref, a_scr):
    @pl.when(pl.program_id(1) == 0)
    def _():
        x = x_ref[...]
        y = x * lax.rsqrt(jnp.mean(x * x, axis=-1, keepdims=True) + EPS) * nw_ref[...]
        a = (y * (1.0 + sc_ref[0]) + sh_ref[0]).astype(BF16)
        a_scr[...] = a
        lat_ref[...] = jnp.dot(a, wl_ref[...], preferred_element_type=F32)

    o_ref[...] = jnp.dot(a_scr[...], w_ref[...], preferred_element_type=F32).astype(o_ref.dtype)


def _in_proj(x, shift, scale, nw, w_lat, w_mg, tm, tn, name):
    r, d = x.shape
    g = shift.shape[0]
    rpg = r // g
    n = w_mg.shape[1]
    nl = w_lat.shape[1]
    tm = _pick(rpg, tm)
    tn = _pick(n, tn, LANE)
    return pl.pallas_call(
        _in_proj_kernel,
        out_shape=(jax.ShapeDtypeStruct((r, nl), F32), jax.ShapeDtypeStruct((r, n), BF16)),
        grid=(r // tm, n // tn),
        in_specs=[pl.BlockSpec((tm, d), lambda i, j: (i, 0)),
                  pl.BlockSpec((1, 1, d), lambda i, j: (i * tm // rpg, 0, 0)),
                  pl.BlockSpec((1, 1, d), lambda i, j: (i * tm // rpg, 0, 0)),
                  pl.BlockSpec((1, d), lambda i, j: (0, 0)),
                  pl.BlockSpec((d, nl), lambda i, j: (0, 0)),
                  pl.BlockSpec((d, tn), lambda i, j: (0, j))],
        out_specs=(pl.BlockSpec((tm, nl), lambda i, j: (i, 0)),
                   pl.BlockSpec((tm, tn), lambda i, j: (i, j))),
        scratch_shapes=[pltpu.VMEM((tm, d), BF16)],
        compiler_params=_cparams(("parallel", "arbitrary")),
        name=name,
    )(x, shift, scale, nw, w_lat, w_mg)


def _rms(x, n):
    return lax.rsqrt(jnp.sum(x * x, axis=-1, keepdims=True) * (1.0 / n) + EPS)


def _mla_prep_kernel(cq_ref, ckv_ref, kr_ref, krp_ref, cos_ref, sin_ref, qna_ref, kvna_ref,
                     wq_ref, wk_ref, wv_ref, qn_ref, qnp_ref, kn_ref, knp_ref,
                     q_out, k_out, v_out):
    cos = cos_ref[...]
    sin = sin_ref[...]
    cq = cq_ref[...]
    cqn = (cq * _rms(cq, MLA_Q_LORA) * qna_ref[...]).astype(BF16)
    qx = jnp.dot(cqn, wq_ref[...], preferred_element_type=F32)
    ckv = ckv_ref[...]
    ckvn = (ckv * _rms(ckv, MLA_KV_LORA) * kvna_ref[...]).astype(BF16)
    kx = jnp.dot(ckvn, wk_ref[...], preferred_element_type=F32)
    vx = jnp.dot(ckvn, wv_ref[...], preferred_element_type=F32)
    lane = lax.broadcasted_iota(jnp.int32, kr_ref.shape, 1)
    krm = jnp.where(lane < MLA_QK, kr_ref[...], 0.0)
    krp = krp_ref[...]
    ssq_kr = jnp.sum(krm * krm, axis=-1, keepdims=True)
    qc = qn_ref[...] * cos
    qs = qnp_ref[...] * sin
    kc = kn_ref[...] * cos
    ks = knp_ref[...] * sin
    hw = MLA_HEADS * LANE
    for h in range(MLA_HEADS):
        sl = slice(h * LANE, (h + 1) * LANE)
        xa = qx[:, sl]
        xb = qx[:, hw + h * LANE: hw + (h + 1) * LANE]
        r = _rms(xa, MLA_QK) * (MLA_QK ** -0.5 * LOG2E)
        q_out[0, h] = ((xa * qc + xb * qs) * r).astype(BF16)
        kn = kx[:, sl]
        rk = lax.rsqrt((jnp.sum(kn * kn, axis=-1, keepdims=True) + ssq_kr) * (1.0 / MLA_QK) + EPS)
        k_out[0, h] = (((kn + krm) * kc + krp * ks) * rk).astype(BF16)
        ones_lane = MLA_V if h % 2 == 0 else 0
        v_out[0, h] = jnp.where(lane == ones_lane, 1.0, vx[:, sl]).astype(BF16)


def _mla_prep(lat, cos_t, sin_t, qna, kvna, wq2, wk, wv, qn, qnp, kn, knp, bsz, seq):
    tm = _pick(seq, 512)
    nb = seq // tm
    hw = MLA_HEADS * LANE
    row = lambda b, s: b * nb + s
    const = lambda b, s: (0, 0)
    oshape = jax.ShapeDtypeStruct((bsz, MLA_HEADS, seq, LANE), BF16)
    ospec = pl.BlockSpec((1, MLA_HEADS, tm, LANE), lambda b, s: (b, 0, s, 0))
    return pl.pallas_call(
        _mla_prep_kernel,
        out_shape=(oshape, oshape, oshape),
        grid=(bsz, nb),
        in_specs=[pl.BlockSpec((tm, 2 * LANE), lambda b, s: (row(b, s), 0)),
                  pl.BlockSpec((tm, LANE), lambda b, s: (row(b, s), 2)),
                  pl.BlockSpec((tm, LANE), lambda b, s: (row(b, s), 3)),
                  pl.BlockSpec((tm, LANE), lambda b, s: (row(b, s), 4)),
                  pl.BlockSpec((tm, LANE), lambda b, s: (s, 0)),
                  pl.BlockSpec((tm, LANE), lambda b, s: (s, 0)),
                  pl.BlockSpec((1, MLA_Q_LORA), const),
                  pl.BlockSpec((1, MLA_KV_LORA), const),
                  pl.BlockSpec((MLA_Q_LORA, 2 * hw), const),
                  pl.BlockSpec((MLA_KV_LORA, hw), const),
                  pl.BlockSpec((MLA_KV_LORA, hw), const),
                  pl.BlockSpec((1, LANE), const), pl.BlockSpec((1, LANE), const),
                  pl.BlockSpec((1, LANE), const), pl.BlockSpec((1, LANE), const)],
        out_specs=(ospec, ospec, ospec),
        compiler_params=_cparams(("parallel", "arbitrary")),
        name="mla_prep",
    )(lat, lat, lat, lat, cos_t, sin_t, qna, kvna, wq2, wk, wv, qn, qnp, kn, knp)


def _attn_kernel(*refs, chunks):
    q_ref, kv, o_ref, s_scr = refs[0], refs[1:-2], refs[-2], refs[-1]
    tq = q_ref.shape[2]
    accs = []
    for hh in range(2):
        q = q_ref[0, hh]

        def scores(c, hh=hh, q=q):
            src, start, size = chunks[c]
            s = _dot_nt(q, kv[2 * src][0, hh, start:start + size, :])
            s_scr[hh, c % 2, :, :size] = s
            return jnp.max(s, axis=-1, keepdims=True)

        m = jnp.full((tq, 1), -jnp.inf, F32)
        acc = jnp.zeros((tq, LANE), F32)
        mt = scores(0)
        for c, (src, start, size) in enumerate(chunks):
            mt_next = scores(c + 1) if c + 1 < len(chunks) else None
            m_new = jnp.maximum(m, mt)
            p = jnp.exp2(s_scr[hh, c % 2, :, :size] - m_new)
            acc = jnp.exp2(m - m_new) * acc + jnp.dot(
                p.astype(BF16), kv[2 * src + 1][0, hh, start:start + size, :],
                preferred_element_type=F32)
            m, mt = m_new, mt_next
        accs.append(acc)
    lane = lax.broadcasted_iota(jnp.int32, (tq, LANE), 1)
    inv0 = 1.0 / accs[0][:, MLA_V:MLA_V + 1]
    inv1 = 1.0 / accs[1][:, 0:1]
    o_ref[0] = jnp.where(lane < MLA_V, accs[0] * inv0, accs[1] * inv1).astype(o_ref.dtype)


def _attention(q, kvs):
    bsz, nh, sq, _ = q.shape
    tq = _pick(sq, 512)
    chunks, in_specs, args = [], [pl.BlockSpec((1, 2, tq, LANE), lambda b, h, i: (b, h, i, 0))], [q]
    for src, (k, v) in enumerate(kvs):
        sk = k.shape[2]
        tk = next(t for t in (2048, 1024, 512, 256, 128, sk) if sk % t == 0)
        chunks += [(src, start, tk) for start in range(0, sk, tk)]
        in_specs += [pl.BlockSpec((1, 2, sk, LANE), lambda b, h, i: (b, h, 0, 0))] * 2
        args += [k, v]
    tk_max = max(size for _, _, size in chunks)
    return pl.pallas_call(
        functools.partial(_attn_kernel, chunks=tuple(chunks)),
        out_shape=jax.ShapeDtypeStruct((bsz, sq, nh * MLA_V), BF16),
        grid=(bsz, nh // 2, sq // tq),
        in_specs=in_specs,
        out_specs=pl.BlockSpec((1, tq, LANE), lambda b, h, i: (b, i, h)),
        scratch_shapes=[pltpu.VMEM((2, 2, tq, tk_max), F32)],
        compiler_params=_cparams(("parallel", "parallel", "arbitrary")),
        name="mla_attention",
    )(*args)


def _dot_nt(a, b):
    return lax.dot_general(a, b, (((1,), (1,)), ((), ())), preferred_element_type=F32)


def _dot_tn(a, b):
    return lax.dot_general(a, b, (((0,), (0,)), ((), ())), preferred_element_type=F32)


def _split_bf16(x, n):
    parts = []
    for _ in range(n):
        p = x.astype(BF16)
        parts.append(p)
        x = x - p.astype(F32)
    return parts


def _dot_exact_lhs(a, x):
    hi, lo = _split_bf16(x, 2)
    return (jnp.dot(a, lo, preferred_element_type=F32)
            + jnp.dot(a, hi, preferred_element_type=F32))


_SCAN_COMBOS = (('gla', False), ('gla', True), ('ret', False), ('ret', True))


def _scans_kernel(*refs, nchunk, nbat):
    n = len(_SCAN_COMBOS)
    ins, outs, scr = refs[:7 * n], refs[7 * n:9 * n], refs[9 * n:10 * n]
    tab_it = iter(refs[10 * n:])
    tabs = [next(tab_it) if kind == 'ret' else None for kind, _ in _SCAN_COMBOS]
    step = pl.program_id(1)
    blk = nchunk * CHUNK

    @pl.when(step == 0)
    def _():
        for i, (kind, reverse) in enumerate(_SCAN_COMBOS):
            scr[i][...] = ins[7 * i + 6][...]
            if kind == 'ret':
                _ret_decay_tables(ins[7 * i + 5], tabs[i], reverse, blk)

    chains = []
    for r in range(nbat):
        for i, (kind, reverse) in enumerate(_SCAN_COMBOS):
            q_ref, k_ref, v_ref, x0_ref, x1_ref, x2_ref = ins[7 * i:7 * i + 6]
            if kind == 'gla':
                x0_ref = x0_ref.at[0, r]
            chains.append(_scan_chain(q_ref.at[0, r], k_ref.at[0, r], v_ref.at[0, r], x0_ref,
                                      x1_ref, x2_ref, outs[2 * i].at[0, r], scr[i].at[r],
                                      tabs[i], kind=kind, reverse=reverse, nchunk=nchunk))
    while chains:
        for chain in list(chains):
            if next(chain, 'done') == 'done':
                chains.remove(chain)

    @pl.when(step == pl.num_programs(1) - 1)
    def _():
        for i in range(n):
            outs[2 * i + 1][...] = scr[i][...]


def _chunk_masks(blk, reverse):
    ii = lax.broadcasted_iota(jnp.int32, (blk, blk), 0)
    jj = lax.broadcasted_iota(jnp.int32, (blk, blk), 1)
    same = (ii // CHUNK) == (jj // CHUNK)
    return same, (same & (jj > ii)) if reverse else (same & (ii >= jj))


def _ret_decay_tables(rd_ref, tab_ref, reverse, blk):
    ii = lax.broadcasted_iota(jnp.int32, (blk, blk), 0)
    jj = lax.broadcasted_iota(jnp.int32, (blk, blk), 1)
    _, amask = _chunk_masks(blk, reverse)
    dd = jnp.where(amask, (jj - ii if reverse else ii - jj).astype(F32), 0.0)
    pidx = (lax.broadcasted_iota(jnp.int32, (blk, LANE), 0) % CHUNK).astype(F32)
    for h in range(4):
        lg = -jnp.exp(rd_ref[h:h + 1, :])
        lgb = jnp.concatenate([lg] * (blk // LANE), axis=1) if blk > LANE else lg[:, :blk]
        tab_ref[h, :, :blk] = jnp.where(amask, jnp.exp(dd * lgb), 0.0)
        if reverse:
            zeta, xi = jnp.exp(pidx * lg), jnp.exp((CHUNK - pidx) * lg)
        else:
            zeta, xi = jnp.exp((CHUNK - 1 - pidx) * lg), jnp.exp((pidx + 1.0) * lg)
        tab_ref[h, :, blk:blk + LANE] = zeta
        tab_ref[h, :, blk + LANE:] = xi


def _scan_chain(q_ref, k_ref, v_ref, x0_ref, x1_ref, x2_ref, o_ref, st_scr, tab_ref, *,
                kind, reverse, nchunk):
    if kind == 'gla':
        r_ref, w2_ref, bg_ref = x0_ref, x1_ref, x2_ref
    else:
        cos_ref, sin_ref, rd_ref = x0_ref, x1_ref, x2_ref

    blk = nchunk * CHUNK
    ii = lax.broadcasted_iota(jnp.int32, (blk, blk), 0)
    jj = lax.broadcasted_iota(jnp.int32, (blk, blk), 1)
    same, amask = _chunk_masks(blk, reverse)
    order = range(nchunk - 1, -1, -1) if reverse else range(nchunk)
    vb = v_ref[...].astype(BF16)

    if kind == 'gla':
        pre = jnp.dot(r_ref[...].astype(BF16), w2_ref[...],
                      preferred_element_type=F32) + bg_ref[...]
        yield
        la = jax.nn.log_sigmoid(pre) * (1.0 / GLA_GATE_NORMALIZER)
        tri = (same & ((jj >= ii) if reverse else (ii >= jj))).astype(BF16)
        cum = _dot_exact_lhs(tri, la)
        yield
        totb = jnp.concatenate(
            [jnp.broadcast_to(cum[c * CHUNK + (0 if reverse else CHUNK - 1)][None, :],
                              (CHUNK, cum.shape[1])) for c in range(nchunk)], axis=0)
        k32 = k_ref[...].astype(F32)
        qd = (q_ref[...].astype(F32) * ((GLA_DK ** -0.5) * jnp.exp(cum))).astype(BF16)
        kd = (k32 * jnp.exp(-cum)).astype(BF16)
        kz = (k32 * jnp.exp(totb - cum)).astype(BF16)
        dtot = jnp.exp(totb)
    else:
        cs = cos_ref[...]
        sn = sin_ref[...]

    for h in range(4):
        cols = slice(h * LANE, (h + 1) * LANE)
        vh = vb[:, cols]
        if kind == 'gla':
            qh, kzh = qd[:, cols], kz[:, cols]
            att = jnp.where(amask, _dot_nt(qh, kd[:, cols]), 0.0)
            xi = None
        else:
            lg = -jnp.exp(rd_ref[h:h + 1, :])
            q = q_ref[:, cols].astype(F32)
            k = k_ref[:, cols].astype(F32)
            qh = (q * cs + pltpu.roll(q, LANE // 2, 1) * sn).astype(BF16)
            kr = (k * cs + pltpu.roll(k, LANE // 2, 1) * sn) * (RET_DK ** -0.5)
            xi = tab_ref[h, :, blk + LANE:]
            kzh = (kr * tab_ref[h, :, blk:blk + LANE]).astype(BF16)
            att = _dot_nt(qh, kr.astype(BF16)) * tab_ref[h, :, :blk]
            gch = jnp.exp(CHUNK * lg)
        yield
        o_intra = jnp.dot(att.astype(BF16), vh, preferred_element_type=F32)
        upd = [_dot_tn(vh[c * CHUNK:(c + 1) * CHUNK], kzh[c * CHUNK:(c + 1) * CHUNK])
               for c in range(nchunk)]
        yield
        st = st_scr[h]
        for c in order:
            rows = slice(c * CHUNK, (c + 1) * CHUNK)
            o = o_intra[rows] + (_dot_nt(qh[rows], st.astype(BF16)) if xi is None
                                 else _dot_nt(qh[rows], st.astype(BF16)) * xi[rows])
            decay = dtot[c * CHUNK:c * CHUNK + 1, cols] if kind == 'gla' else gch
            st = st * decay + upd[c]
            o_ref[rows, cols] = o
        st_scr[h] = st
        yield


def _scans(mg, lat, bsz, seq, states, gla_extra, ret_extra):
    blk = _pick(seq, 256, CHUNK)
    nb = seq // blk
    nbat = 2 if bsz % 2 == 0 else 1
    const = lambda b, s: (0, 0)
    sspec = pl.BlockSpec((nbat, 4, LANE, LANE), lambda b, s: (b, 0, 0, 0))
    mg4 = mg.reshape(bsz // nbat, nbat, seq, mg.shape[1])
    lat4 = lat.reshape(bsz // nbat, nbat, seq, lat.shape[1])
    in_specs, args, out_specs = [], [], []
    for (kind, reverse), s0 in zip(_SCAN_COMBOS, states):
        pos = (lambda s: nb - 1 - s) if reverse else (lambda s: s)
        col0 = 0 if kind == 'gla' else 4
        in_specs += [pl.BlockSpec((1, nbat, blk, 512),
                                  lambda b, s, c=col0 + j, pos=pos: (b, 0, pos(s), c))
                     for j in range(3)]
        args += [mg4, mg4, mg4]
        if kind == 'gla':
            in_specs += [pl.BlockSpec((1, nbat, blk, LANE), lambda b, s, pos=pos: (b, 0, pos(s), 3)),
                         pl.BlockSpec((LANE, 512), const), pl.BlockSpec((1, 512), const)]
            args += [lat4, *gla_extra[reverse]]
        else:
            in_specs += [pl.BlockSpec((blk, LANE), lambda b, s, pos=pos: (pos(s), 0)),
                         pl.BlockSpec((blk, LANE), lambda b, s, pos=pos: (pos(s), 0)),
                         pl.BlockSpec((SUBLANE, LANE), const)]
            args += [ret_extra[0], ret_extra[1], ret_extra[2][reverse]]
        in_specs.append(sspec)
        args.append(s0)
        out_specs += [pl.BlockSpec((1, nbat, blk, 512), lambda b, s, pos=pos: (b, 0, pos(s), 0)),
                      sspec]
    n = len(_SCAN_COMBOS)
    outs = pl.pallas_call(
        functools.partial(_scans_kernel, nchunk=blk // CHUNK, nbat=nbat),
        out_shape=(jax.ShapeDtypeStruct((bsz // nbat, nbat, seq, 512), F32),
                   jax.ShapeDtypeStruct((bsz, 4, LANE, LANE), F32)) * n,
        grid=(bsz // nbat, nb),
        in_specs=in_specs,
        out_specs=out_specs,
        scratch_shapes=([pltpu.VMEM((nbat, 4, LANE, LANE), F32)] * n
                        + [pltpu.VMEM((4, blk, blk + 2 * LANE), F32)
                           for kind, _ in _SCAN_COMBOS if kind == 'ret']),
        compiler_params=_cparams(("parallel", "arbitrary")),
        name="recurrent_scans",
    )(*args)
    return [o.reshape(bsz * seq, 512) for o in outs[0::2]], list(outs[1::2])


def _merge_kernel(ym_ref, ogf_ref, ogb_ref, orf_ref, orb_ref, gg_ref, gr_ref, g0_ref, g1_ref,
                  g2_ref, on_ref, bg_ref, wb_ref, wo_ref, h_ref, gt_ref, o_ref):
    def gated_head_norm(of_ref, ob_ref, g_ref, w, rows):
        parts = []
        for hd in range(4):
            cols = slice(hd * LANE, (hd + 1) * LANE)
            osum = of_ref[rows, cols] + ob_ref[rows, cols]
            y = osum * _rms(osum, LANE)
            if w is not None:
                y = y * w
            g = g_ref[rows, cols].astype(F32)
            parts.append((y * (g * jax.nn.sigmoid(g))).astype(BF16))
        return jnp.concatenate(parts, axis=1)

    def rows_chain(rows):
        ys = (ym_ref[rows, :], gated_head_norm(ogf_ref, ogb_ref, gg_ref, on_ref[...], rows),
              gated_head_norm(orf_ref, orb_ref, gr_ref, None, rows))
        ts = [jnp.dot(y, wb_ref[n], preferred_element_type=F32) for n, y in enumerate(ys)]
        yield
        u = None
        for n, g_ref in enumerate((g0_ref, g1_ref, g2_ref)):
            cols = slice(n * D_MODEL, (n + 1) * D_MODEL)
            t = jax.nn.sigmoid(g_ref[rows, :].astype(F32) + bg_ref[:, cols]) * ts[n]
            u = t if u is None else u + t
        out = jnp.dot(u.astype(BF16), wo_ref[...], preferred_element_type=F32)
        o_ref[rows, :] = h_ref[rows, :] + gt_ref[0] * out
        yield

    tm = h_ref.shape[0]
    sub = tm // 2 if tm % (2 * HALO) == 0 else tm
    chains = [rows_chain(slice(r0, r0 + sub)) for r0 in range(0, tm, sub)]
    while chains:
        for chain in list(chains):
            if next(chain, 'done') == 'done':
                chains.remove(chain)


def _merge(y_mla, scan_o, mg, on, bg, wb, wo, h, gt):
    r, d = h.shape
    g = gt.shape[0]
    rpg = r // g
    tm = _pick(rpg, 512)
    yspec = pl.BlockSpec((tm, BRANCH_W), lambda i: (i, 0))
    gspec = lambda c: pl.BlockSpec((tm, BRANCH_W), lambda i, c=c: (i, c))
    mspec = lambda c: pl.BlockSpec((tm, d), lambda i, c=c: (i, c))
    return pl.pallas_call(
        _merge_kernel,
        out_shape=jax.ShapeDtypeStruct((r, d), F32),
        grid=(r // tm,),
        in_specs=[yspec] * 5 + [gspec(3), gspec(7), mspec(4), mspec(5), mspec(6),
                  pl.BlockSpec((1, LANE), lambda i: (0, 0)),
                  pl.BlockSpec((1, GATE_W), lambda i: (0, 0)),
                  pl.BlockSpec((N_BRANCH, BRANCH_W, d), lambda i: (0, 0, 0)),
                  pl.BlockSpec((d, d), lambda i: (0, 0)),
                  pl.BlockSpec((tm, d), lambda i: (i, 0)),
                  pl.BlockSpec((1, 1, d), lambda i: (i * tm // rpg, 0, 0))],
        out_specs=pl.BlockSpec((tm, d), lambda i: (i, 0)),
        compiler_params=_cparams(("parallel",)),
        name="gated_merge",
    )(y_mla, *scan_o, mg, mg, mg, mg, mg, on, bg, wb, wo, h, gt)


def _ffn_kernel(hp_ref, h_ref, hn_ref, sh_ref, sc_ref, gt_ref, nw_ref, wg_ref, wu_ref, wdw_ref,
                bdw_ref, wo_ref, o_ref, act_scr, *, tm, seq, tf):
    i = pl.program_id(0)

    def norm_mod(x):
        y = x * lax.rsqrt(jnp.mean(x * x, axis=-1, keepdims=True) + EPS) * nw_ref[...]
        return y * (1.0 + sc_ref[0]) + sh_ref[0]

    row0 = i * tm
    prev_ok = (row0 % seq != 0).astype(F32)
    next_ok = ((row0 + tm) % seq != 0).astype(F32)

    a_mid = norm_mod(h_ref[...]).astype(BF16)
    a_ext = jnp.concatenate([(norm_mod(hp_ref[...]) * prev_ok).astype(BF16), a_mid,
                             (norm_mod(hn_ref[...]) * next_ok).astype(BF16)], axis=0)
    nchunk = wg_ref.shape[1] // tf

    def hidden(j):
        cs = slice(j * tf, (j + 1) * tf)
        gate = jnp.dot(a_ext, wg_ref[:, cs], preferred_element_type=F32)
        up = jnp.dot(a_mid, wu_ref[:, cs], preferred_element_type=F32)
        return gate, up

    for j in range(nchunk):
        cs = slice(j * tf, (j + 1) * tf)
        gate, up = hidden(j)
        g_prev = pltpu.roll(gate, 1, 0)[HALO:HALO + tm]
        g_next = pltpu.roll(gate, tm + 2 * HALO - 1, 0)[HALO:HALO + tm]
        conv = (g_prev * wdw_ref[0:1, cs] + gate[HALO:HALO + tm] * wdw_ref[1:2, cs]
                + g_next * wdw_ref[2:3, cs] + bdw_ref[:, cs])
        act_scr[:, cs] = (jax.nn.gelu(conv, approximate=True) * up).astype(BF16)
    out = jnp.dot(act_scr[...], wo_ref[...], preferred_element_type=F32)
    o_ref[...] = h_ref[...] + gt_ref[0] * out


def _ffn(h, shift, scale, gt, nw, wg, wu, wdw, bdw, wo, seq):
    r, d = h.shape
    g = shift.shape[0]
    rpg = r // g
    dff = wg.shape[1]
    tm = _pick(seq, 1024, HALO)
    tf = _pick(dff, 2 * LANE, LANE)
    nsub = tm // HALO
    nrb = r // HALO
    gidx = lambda i: (i * tm // rpg, 0, 0)
    const = lambda i: (0, 0)
    kern = functools.partial(_ffn_kernel, tm=tm, seq=seq, tf=tf)
    return pl.pallas_call(
        kern,
        out_shape=jax.ShapeDtypeStruct((r, d), F32),
        grid=(r // tm,),
        in_specs=[pl.BlockSpec((HALO, d), lambda i: (jnp.maximum(i * nsub - 1, 0), 0)),
                  pl.BlockSpec((tm, d), lambda i: (i, 0)),
                  pl.BlockSpec((HALO, d), lambda i: (jnp.minimum((i + 1) * nsub, nrb - 1), 0)),
                  pl.BlockSpec((1, 1, d), gidx), pl.BlockSpec((1, 1, d), gidx),
                  pl.BlockSpec((1, 1, d), gidx),
                  pl.BlockSpec((1, d), const),
                  pl.BlockSpec((d, dff), const, pipeline_mode=pl.Buffered(1)),
                  pl.BlockSpec((d, dff), const, pipeline_mode=pl.Buffered(1)),
                  pl.BlockSpec((3, dff), const),
                  pl.BlockSpec((1, dff), const),
                  pl.BlockSpec((dff, d), const, pipeline_mode=pl.Buffered(1))],
        out_specs=pl.BlockSpec((tm, d), lambda i: (i, 0)),
        scratch_shapes=[pltpu.VMEM((tm, dff), BF16)],
        compiler_params=_cparams(("parallel",)),
        name="conv_ffn",
    )(h, h, h, shift, scale, gt, nw, wg, wu, wdw, bdw, wo)


def _mla_weight_layout():
    pidx, psgn = _rope_partner(MLA_ROPE)
    qa_idx, qa_sgn, qb_idx, qb_sgn = [], [], [], []
    k_idx, k_sgn, v_idx, v_sgn = [], [], [], []
    for h in range(MLA_HEADS):
        base = h * MLA_QK
        qa_idx += list(range(base, base + MLA_QK)) + [0] * 32
        qa_sgn += [1.0] * MLA_QK + [0.0] * 32
        qb_idx += [0] * 64 + (base + MLA_NOPE + pidx).tolist() + [0] * 32
        qb_sgn += [0.0] * 64 + psgn.tolist() + [0.0] * 32
        kb = h * (MLA_NOPE + MLA_V)
        k_idx += list(range(kb, kb + MLA_NOPE)) + [0] * 64
        k_sgn += [1.0] * 64 + [0.0] * 64
        vcols = list(range(kb + MLA_NOPE, kb + MLA_NOPE + MLA_V))
        if h % 2 == 0:
            v_idx += vcols + [0] * 64
            v_sgn += [1.0] * 64 + [0.0] * 64
        else:
            v_idx += [0] * 64 + vcols
            v_sgn += [0.0] * 64 + [1.0] * 64
    f = lambda a, t: np.asarray(a, t)
    return (f(qa_idx + qb_idx, np.int32), f(qa_sgn + qb_sgn, np.float32),
            f(k_idx, np.int32), f(k_sgn, np.float32), f(v_idx, np.int32), f(v_sgn, np.float32))


def _slot_vec(w):
    pidx, _ = _rope_partner(MLA_ROPE)
    z32 = jnp.zeros((32,), F32)
    a = jnp.concatenate([w, z32])
    p = jnp.concatenate([jnp.zeros((64,), F32), w[MLA_NOPE + pidx], z32])
    return a.reshape(1, LANE), p.reshape(1, LANE)


def _mla_tables(seq, rope):
    ones = jnp.ones((seq, 64), F32)
    if not rope:
        return jnp.ones((seq, LANE), F32), jnp.zeros((seq, LANE), F32)
    pos = jnp.arange(seq)
    dim = MLA_ROPE // 2
    inv = ROPE_THETA ** (-jnp.arange(dim // 2, dtype=F32) * 2.0 / dim)
    ar = (pos // GRID_W).astype(F32)[:, None] * inv[None, :]
    ac = (pos % GRID_W).astype(F32)[:, None] * inv[None, :]
    cos = jnp.concatenate([ones, jnp.cos(ar), jnp.cos(ar), jnp.cos(ac), jnp.cos(ac),
                           jnp.ones((seq, 32), F32)], axis=1)
    sin = jnp.concatenate([0.0 * ones, jnp.sin(ar), jnp.sin(ar), jnp.sin(ac), jnp.sin(ac),
                           jnp.zeros((seq, 32), F32)], axis=1)
    return cos, sin


def _ret_tables(start, seq):
    inv = 1.0 / (RET_THETA ** jnp.linspace(0.0, 1.0, RET_DK // 2, dtype=F32))
    ang = (start + jnp.arange(seq)).astype(F32)[:, None] * inv[None, :]
    cos, sin = jnp.cos(ang), jnp.sin(ang)
    return jnp.concatenate([cos, cos], axis=1), jnp.concatenate([-sin, sin], axis=1)


def _split_in_proj(w):
    o = _IN_OFF
    z = lambda n: jnp.zeros((w.shape[0], n), w.dtype)
    kr = o['mla_kr']
    q8 = MLA_ROPE // 4
    partner = []
    for base in (kr, kr + 2 * q8):
        partner += [-w[:, base + q8:base + 2 * q8], w[:, base:base + q8]]
    w_lat = jnp.concatenate(
        [w[:, :kr], z(64), w[:, kr:kr + MLA_ROPE], w[:, o['gla_rf']:o['gla_rf'] + 2 * GLA_GATE_RANK],
         z(64)] + partner + [z(32)], axis=1)
    w_mg = jnp.concatenate([w[:, o['gla_q']:o['gla_rf']], w[:, o['ret_q']:]], axis=1)
    return w_lat.astype(BF16), w_mg.astype(BF16)


def _mixers(lat, mg, bsz, seq, lw, mla_tabs, ret_tabs, states):
    qkv = _mla_prep(lat, mla_tabs[0], mla_tabs[1], lw['qna'], lw['kvna'], lw['wq2'], lw['wk'],
                    lw['wv'], lw['qn'], lw['qnp'], lw['kn'], lw['knp'], bsz, seq)
    scan_o, finals = _scans(mg, lat, bsz, seq, states,
                            {False: (lw['w2f'], lw['bgf']), True: (lw['w2b'], lw['bgb'])},
                            (ret_tabs[0], ret_tabs[1], {False: lw['rd_f'], True: lw['rd_b']}))
    return qkv, scan_o, finals


def kernel(x, c, ctx, c_ctx, w_ada, b_ada, norm1_w, norm2_w, w_in, b_gate, mla_q_norm_a, mla_w_qb,
           mla_kv_norm_a, mla_w_kvb, mla_q_norm, mla_k_norm, gla_w_gk2, gla_b_gk, gla_o_norm,
           ret_decay, w_branch, w_out, w_ffn_in, w_dw, b_dw, w_ffn_out):
    bsz, seq, d = x.shape
    clen = ctx.shape[1]
    depth = w_ada.shape[0]
    r_lat, r_ctx = bsz * seq, bsz * clen

    npad = -(bsz + 1) % SUBLANE
    cc = jnp.concatenate([c, c_ctx[None, :], jnp.zeros((npad, d), F32)], axis=0)
    mod_all = _ada_mod(cc, w_ada, b_ada)

    qidx, qsgn, kidx, ksgn, vidx, vsgn = _mla_weight_layout()
    mla_lat_tabs = _mla_tables(seq, True)
    mla_ctx_tabs = _mla_tables(clen, False)
    ret_lat_tabs = _ret_tables(clen, seq)
    ret_ctx_tabs = _ret_tables(0, clen)
    zstate = jnp.zeros((bsz, 4, LANE, LANE), F32)

    h = x.reshape(r_lat, d)
    hc = ctx.reshape(r_ctx, d)
    for l in range(depth):
        need_ctx = l < depth - 1
        mods = mod_all[l].reshape(-1, 6, d)
        ml = [mods[:bsz, j].reshape(bsz, 1, d) for j in range(6)]
        mc = [mods[bsz:bsz + 1, j].reshape(1, 1, d) for j in range(6)]

        w_lat, w_mg = _split_in_proj(w_in[l])
        qn, qnp = _slot_vec(mla_q_norm[l])
        kn, knp = _slot_vec(mla_k_norm[l])
        w2 = gla_w_gk2[l]
        zr = jnp.zeros((LANE - 2 * GLA_GATE_RANK, 4 * GLA_DK), F32)
        z16 = jnp.zeros((GLA_GATE_RANK, 4 * GLA_DK), F32)
        lw = dict(
            qna=mla_q_norm_a[l].reshape(1, -1), kvna=mla_kv_norm_a[l].reshape(1, -1),
            wq2=(mla_w_qb[l][:, qidx] * qsgn[None, :]).astype(BF16),
            wk=(mla_w_kvb[l][:, kidx] * ksgn[None, :]).astype(BF16),
            wv=(mla_w_kvb[l][:, vidx] * vsgn[None, :]).astype(BF16),
            qn=qn, qnp=qnp, kn=kn, knp=knp,
            w2f=jnp.concatenate([zr, w2[0], z16], axis=0).astype(BF16),
            w2b=jnp.concatenate([zr, z16, w2[1]], axis=0).astype(BF16),
            bgf=gla_b_gk[l, 0].reshape(1, -1), bgb=gla_b_gk[l, 1].reshape(1, -1),
            gla_on=gla_o_norm[l].reshape(1, LANE),
            rd_f=jnp.broadcast_to(jnp.pad(ret_decay[l, 0], (0, 4))[:, None], (SUBLANE, LANE)),
            rd_b=jnp.broadcast_to(jnp.pad(ret_decay[l, 1], (0, 4))[:, None], (SUBLANE, LANE)),
        )
        nw1 = norm1_w[l].reshape(1, d)
        nw2 = norm2_w[l].reshape(1, d)
        bg = b_gate[l].reshape(1, GATE_W)
        wb = w_branch[l].astype(BF16)
        wo = w_out[l].astype(BF16)
        wg = w_ffn_in[l][:, :D_FF].astype(BF16)
        wu = w_ffn_in[l][:, D_FF:].astype(BF16)
        wfo = w_ffn_out[l].astype(BF16)
        bdw = b_dw[l].reshape(1, D_FF)

        def project(hh, m, name):
            return _in_proj(hh, m[0], m[1], nw1, w_lat, w_mg, 2048, 1024, name)

        lat_c, mg_c = project(hc, mc, "in_proj_ctx")
        (q_c, k_c, v_c), so_c, states = _mixers(
            lat_c, mg_c, bsz, clen, lw, mla_ctx_tabs, ret_ctx_tabs, (zstate,) * 4)
        lat_l, mg_l = project(h, ml, "in_proj")
        (q_l, k_l, v_l), so_l, _ = _mixers(
            lat_l, mg_l, bsz, seq, lw, mla_lat_tabs, ret_lat_tabs, states)
        y_mla = _attention(q_l, ((k_c, v_c), (k_l, v_l))).reshape(r_lat, BRANCH_W)
        h = _merge(y_mla, so_l, mg_l, lw['gla_on'], bg, wb, wo, h, ml[2])
        h = _ffn(h, ml[3], ml[4], ml[5], nw2, wg, wu, w_dw[l], bdw, wfo, seq)
        if need_ctx:
            ym_c = _attention(q_c, ((k_c, v_c),)).reshape(r_ctx, BRANCH_W)
            hc = _merge(ym_c, so_c, mg_c, lw['gla_on'], bg, wb, wo, hc, mc[2])
            hc = _ffn(hc, mc[3], mc[4], mc[5], nw2, wg, wu, w_dw[l], bdw, wfo, clen)
    return h.reshape(bsz, seq, d)
```
